```python
import jax
import jax.numpy as jnp
from jax import lax
import numpy as np


D_MODEL = 1024
BATCH = 32
SEQ = 2048
DEPTH = 1

D_MIX = D_MODEL
D_ATTN = D_MIX // 2
D_HGRN = D_MIX - D_ATTN
ATTN_HEAD_DIM = 64
N_ATTN_HEADS = D_ATTN // ATTN_HEAD_DIM
HGRN_EXPAND = 128
N_HGRN_HEADS = D_HGRN // HGRN_EXPAND
HGRN_VDIM = D_HGRN // N_HGRN_HEADS
D_FF = 128 * ((8 * D_MODEL // 3 + 127) // 128)
N_MOD = 9
Q_BLOCK = 128
HGRN_CHUNK = 64
FFN_RES = 0.5
EPS = 1e-6
IN_SIZES = (D_ATTN, D_ATTN, D_ATTN, N_ATTN_HEADS, D_HGRN, D_HGRN, D_HGRN, D_HGRN)
N_IN = sum(IN_SIZES)

kernel_name = "fox_hgrn2_macaron_adaln_hybrid"


def _rms(x, g):
    xf = x.astype(jnp.float32)
    y = xf * lax.rsqrt(jnp.mean(xf * xf, axis=-1, keepdims=True) + EPS)
    return (y * g.astype(jnp.float32)).astype(x.dtype)


def _modulate(x, g, shift, scale):
    return _rms(x, g) * (1 + scale[:, None, :]) + shift[:, None, :]


def _swiglu(h, w_in, w_out):
    gate, up = jnp.split(h @ w_in, 2, axis=-1)
    return (jax.nn.silu(gate) * up) @ w_out


def _forgetting_attention(q, k, v, fg_logit, q_g, k_g):
    B, S, H, Dh = q.shape
    q = _rms(q, q_g).transpose(0, 2, 1, 3)
    k = _rms(k, k_g).transpose(0, 2, 1, 3)
    v = v.transpose(0, 2, 1, 3)
    logf = jax.nn.log_sigmoid(fg_logit.astype(jnp.float32))
    F = jnp.cumsum(logf, axis=1).transpose(0, 2, 1)
    scale = Dh ** -0.5
    outs = []
    for blk in range(S // Q_BLOCK):
        q0 = blk * Q_BLOCK
        L = q0 + Q_BLOCK
        s = jnp.einsum('bhqd,bhkd->bhqk', q[:, :, q0:L], k[:, :, :L],
                       preferred_element_type=jnp.float32) * scale
        s = s + F[:, :, q0:L, None] - F[:, :, None, :L]
        qpos = q0 + jnp.arange(Q_BLOCK)
        kpos = jnp.arange(L)
        s = jnp.where(kpos[None, :] <= qpos[:, None], s, -jnp.inf)
        p = jax.nn.softmax(s, axis=-1)
        outs.append(jnp.einsum('bhqk,bhkd->bhqd', p.astype(v.dtype), v[:, :, :L]))
    o = jnp.concatenate(outs, axis=2)
    return o.transpose(0, 2, 1, 3).reshape(B, S, H * Dh)


def _hgrn2(q, f_logit, i, lb):
    B, S, H, K = q.shape
    V = i.shape[-1]
    z = f_logit.astype(jnp.float32)
    lbf = lb.astype(jnp.float32).reshape(H, K)
    logf = jnp.logaddexp(jnp.log(lbf), jnp.log1p(-lbf) + jax.nn.log_sigmoid(z))
    kk = (1 - lbf) * jax.nn.sigmoid(-z)
    qf = jax.nn.silu(q.astype(jnp.float32))
    C = HGRN_CHUNK
    NC = S // C

    def to_chunks(t):
        return t.reshape(B, NC, C, H, t.shape[-1]).transpose(1, 0, 3, 2, 4)

    xs = (to_chunks(qf), to_chunks(kk), to_chunks(i.astype(jnp.float32)), to_chunks(logf))
    causal = jnp.tril(jnp.ones((C, C), dtype=bool))

    def step(state, inp):
        qb, kb, vb, gb = inp
        b = jnp.cumsum(gb, axis=2)
        o_inter = jnp.einsum('bhck,bhkv->bhcv', qb * jnp.exp(b), state)
        diff = b[:, :, :, None, :] - b[:, :, None, :, :]
        decay = jnp.exp(jnp.where(causal[:, :, None], diff, -jnp.inf))
        A = jnp.einsum('bhtk,bhsk,bhtsk->bhts', qb, kb, decay)
        o_intra = jnp.einsum('bhts,bhsv->bhtv', A, vb)
        b_last = b[:, :, -1:, :]
        new_state = (jnp.exp(b_last[:, :, 0, :])[..., None] * state
                     + jnp.einsum('bhsk,bhsv->bhkv', kb * jnp.exp(b_last - b), vb))
        return new_state, o_inter + o_intra

    s0 = jnp.zeros((B, H, K, V), jnp.float32)
    _, o = lax.scan(step, s0, xs)
    return o.transpose(1, 0, 3, 2, 4).reshape(B, S, H, V).astype(q.dtype)


def _hybrid_mixer(h, w_in, b_fgate, q_g, k_g, attn_out_g, lb, hgrn_out_g, w_out):
    B, S, _ = h.shape
    proj = h @ w_in
    idx = np.cumsum(IN_SIZES)[:-1].tolist()
    qa, ka, va, fg, qh, fh, ih, gh = jnp.split(proj, idx, axis=-1)
    heads_a = lambda t: t.reshape(B, S, N_ATTN_HEADS, ATTN_HEAD_DIM)
    o_a = _forgetting_attention(heads_a(qa), heads_a(ka), heads_a(va), fg + b_fgate, q_g, k_g)
    o_a = _rms(o_a, attn_out_g)
    heads_k = lambda t: t.reshape(B, S, N_HGRN_HEADS, HGRN_EXPAND)
    o_h = _hgrn2(heads_k(qh), heads_k(fh), ih.reshape(B, S, N_HGRN_HEADS, HGRN_VDIM), lb)
    o_h = _rms(o_h, hgrn_out_g.reshape(N_HGRN_HEADS, HGRN_VDIM)).reshape(B, S, D_HGRN)
    o_h = o_h * jax.nn.silu(gh)
    return jnp.concatenate([o_a, o_h], axis=-1) @ w_out


def setup_inputs(seed: int = 0) -> dict:
    key = jax.random.key(seed)
    ks = jax.random.split(key, 24)
    f32 = jnp.float32
    nrm = lambda k, shape, s: s * jax.random.normal(k, shape, f32)
    L = DEPTH
    w_in_mix = jnp.concatenate([
        nrm(ks[9], (L, D_MODEL, 3 * D_ATTN), D_MODEL ** -0.5),
        nrm(ks[10], (L, D_MODEL, N_ATTN_HEADS), 0.1 * D_MODEL ** -0.5),
        nrm(ks[11], (L, D_MODEL, 4 * D_HGRN), D_MODEL ** -0.5)], axis=-1)
    return {
        'x': nrm(ks[0], (BATCH, SEQ, D_MODEL), 1.0),
        'c': nrm(ks[1], (BATCH, D_MODEL), 1.0),
        'w_ada': nrm(ks[2], (L, D_MODEL, N_MOD * D_MODEL), 0.5 * D_MODEL ** -0.5),
        'b_ada': nrm(ks[3], (L, N_MOD * D_MODEL), 0.02),
        'g_norm1': 1 + nrm(ks[4], (L, D_MODEL), 0.02),
        'ffn1_w_in': nrm(ks[5], (L, D_MODEL, 2 * D_FF), D_MODEL ** -0.5),
        'ffn1_w_out': nrm(ks[6], (L, D_FF, D_MODEL), D_FF ** -0.5),
        'g_norm_mix': 1 + nrm(ks[7], (L, D_MODEL), 0.02),
        'w_in_mix': w_in_mix,
        'b_fgate': jax.random.uniform(ks[12], (L, N_ATTN_HEADS), f32, 1.0, 4.0),
        'q_norm_g': 1 + nrm(ks[13], (L, ATTN_HEAD_DIM), 0.02),
        'k_norm_g': 1 + nrm(ks[14], (L, ATTN_HEAD_DIM), 0.02),
        'attn_out_g': 1 + nrm(ks[15], (L, D_ATTN), 0.02),
        'hgrn_lb_logits': nrm(ks[16], (DEPTH + 1, D_HGRN), 0.1),
        'hgrn_out_g': 1 + nrm(ks[17], (L, D_HGRN), 0.02),
        'w_out_mix': nrm(ks[18], (L, D_MIX, D_MODEL), D_MIX ** -0.5),
        'g_norm2': 1 + nrm(ks[19], (L, D_MODEL), 0.02),
        'ffn2_w_in': nrm(ks[20], (L, D_MODEL, 2 * D_FF), D_MODEL ** -0.5),
        'ffn2_w_out': nrm(ks[21], (L, D_FF, D_MODEL), D_FF ** -0.5),
    }


def reference(x, c, w_ada, b_ada, g_norm1, ffn1_w_in, ffn1_w_out, g_norm_mix, w_in_mix,
              b_fgate, q_norm_g, k_norm_g, attn_out_g, hgrn_lb_logits, hgrn_out_g,
              w_out_mix, g_norm2, ffn2_w_in, ffn2_w_out):
    lb_all = jnp.cumsum(jax.nn.softmax(hgrn_lb_logits.astype(jnp.float32), axis=0), axis=0)
    cs = jax.nn.silu(c)
    for l in range(DEPTH):
        mod = cs @ w_ada[l] + b_ada[l]
        sh1, sc1, gt1, shm, scm, gtm, sh2, sc2, gt2 = jnp.split(mod, N_MOD, axis=-1)
        h = _modulate(x, g_norm1[l], sh1, sc1)
        x = x + FFN_RES * gt1[:, None, :] * _swiglu(h, ffn1_w_in[l], ffn1_w_out[l])
        h = _modulate(x, g_norm_mix[l], shm, scm)
        x = x + gtm[:, None, :] * _hybrid_mixer(h, w_in_mix[l], b_fgate[l], q_norm_g[l], k_norm_g[l],
                                                attn_out_g[l], lb_all[l], hgrn_out_g[l], w_out_mix[l])
        h = _modulate(x, g_norm2[l], sh2, sc2)
        x = x + FFN_RES * gt2[:, None, :] * _swiglu(h, ffn2_w_in[l], ffn2_w_out[l])
    return x
```

```python
import functools

import jax
import jax.numpy as jnp
from jax import lax
from jax.experimental import pallas as pl
from jax.experimental.pallas import tpu as pltpu

D_MODEL = 1024
BATCH = 32
SEQ = 2048
D_ATTN = 512
D_HGRN = 512
ATTN_HEAD_DIM = 64
N_ATTN_HEADS = 8
HGRN_K = 128
N_HGRN_HEADS = 4
D_FF = 2816
N_MOD = 9
FFN_RES = 0.5
EPS = 1e-6

LANES = 128
SUBLANES = 8
N_MAIN = 3 * D_ATTN + 4 * D_HGRN

F32 = jnp.float32
BF16 = jnp.bfloat16

TM_FFN = 512
TM_MIX = 512
TQ = 256
TK = 256
HGRN_C = 64
VMEM_LIMIT = 56 * 1024 * 1024


def _sigmoid(x):
    return 1.0 / (1.0 + jnp.exp(-x))


def _silu(x):
    return x * _sigmoid(x)


def _modulated_rms(x, g, shift, scale):
    ms = jnp.mean(x * x, axis=-1, keepdims=True)
    y = x * lax.rsqrt(ms + EPS) * g
    return y * (1.0 + scale) + shift


def _adaln_kernel(c_ref, w_ref, b_ref, o_ref):
    cs = _silu(c_ref[...]).astype(BF16)
    w = w_ref[...].astype(BF16)
    o_ref[...] = jnp.dot(cs, w, preferred_element_type=F32) + b_ref[...]


def _adaln(c, w, b):
    n = w.shape[1]
    tn = D_MODEL
    return pl.pallas_call(
        _adaln_kernel,
        grid=(n // tn,),
        in_specs=[
            pl.BlockSpec((BATCH, D_MODEL), lambda j: (0, 0)),
            pl.BlockSpec((D_MODEL, tn), lambda j: (0, j)),
            pl.BlockSpec((1, tn), lambda j: (0, j)),
        ],
        out_specs=pl.BlockSpec((BATCH, tn), lambda j: (0, j)),
        out_shape=jax.ShapeDtypeStruct((BATCH, n), F32),
        compiler_params=pltpu.CompilerParams(dimension_semantics=("arbitrary",)),
        name="adaln",
    )(c, w, b)


def _ffn_kernel(x_ref, mod_ref, g_ref, win_ref, wout_ref, o_ref, *, mod_base):
    x = x_ref[...]
    shift = mod_ref[0, mod_base:mod_base + 1, :]
    scale = mod_ref[0, mod_base + 1:mod_base + 2, :]
    gate = mod_ref[0, mod_base + 2:mod_base + 3, :]
    h = _modulated_rms(x, g_ref[...], shift, scale).astype(BF16)
    gu = jnp.dot(h, win_ref[...], preferred_element_type=F32)
    a = (_silu(gu[:, :D_FF]) * gu[:, D_FF:]).astype(BF16)
    out = jnp.dot(a, wout_ref[...], preferred_element_type=F32)
    o_ref[...] = x + (FFN_RES * gate) * out


def _resident(shape):
    return pl.BlockSpec(shape, lambda i: (0,) * len(shape), pipeline_mode=pl.Buffered(1))


def _ffn(x2d, mod, g, w_in, w_out, mod_base, name):
    t = x2d.shape[0]
    tiles_per_seq = SEQ // TM_FFN
    return pl.pallas_call(
        functools.partial(_ffn_kernel, mod_base=mod_base),
        grid=(t // TM_FFN,),
        in_specs=[
            pl.BlockSpec((TM_FFN, D_MODEL), lambda i: (i, 0)),
            pl.BlockSpec((1, N_MOD, D_MODEL), lambda i: (i // tiles_per_seq, 0, 0)),
            _resident((1, D_MODEL)),
            _resident((D_MODEL, 2 * D_FF)),
            _resident((D_FF, D_MODEL)),
        ],
        out_specs=pl.BlockSpec((TM_FFN, D_MODEL), lambda i: (i, 0)),
        out_shape=jax.ShapeDtypeStruct((t, D_MODEL), F32),
        compiler_params=pltpu.CompilerParams(
            dimension_semantics=("arbitrary",), vmem_limit_bytes=VMEM_LIMIT),
        name=name,
    )(x2d, mod, g, w_in, w_out)


def _mix_in_kernel(x_ref, mod_ref, g_ref, w_ref, wfg_ref, bfg_ref, qg_ref, kg_ref, lbl_ref,
                   qa_ref, ka_ref, va_ref, ls_ref, qf_ref, lf_ref, kk_ref, iv_ref, gt_ref):
    x = x_ref[...]
    h = _modulated_rms(x, g_ref[...], mod_ref[0, 3:4, :], mod_ref[0, 4:5, :]).astype(BF16)
    p = jnp.dot(h, w_ref[...], preferred_element_type=F32)

    lo = lax.broadcasted_iota(jnp.int32, (1, LANES), 1) < ATTN_HEAD_DIM
    inv_dh = 1.0 / ATTN_HEAD_DIM

    def head_norm(t, gain):
        sq = t * t
        s_lo = jnp.sum(jnp.where(lo, sq, 0.0), axis=-1, keepdims=True)
        s_hi = jnp.sum(jnp.where(lo, 0.0, sq), axis=-1, keepdims=True)
        r = jnp.where(lo, lax.rsqrt(s_lo * inv_dh + EPS), lax.rsqrt(s_hi * inv_dh + EPS))
        return (t * r * gain).astype(BF16)

    q_gain = qg_ref[...] * (ATTN_HEAD_DIM ** -0.5)
    k_gain = kg_ref[...]
    for j in range(D_ATTN // LANES):
        sl = slice(j * LANES, (j + 1) * LANES)
        qa_ref[:, sl] = head_norm(p[:, j * LANES:(j + 1) * LANES], q_gain)
        ka_ref[:, sl] = head_norm(p[:, D_ATTN + j * LANES:D_ATTN + (j + 1) * LANES], k_gain)
    va_ref[...] = p[:, 2 * D_ATTN:3 * D_ATTN].astype(BF16)

    fg = jnp.dot(h, wfg_ref[...], preferred_element_type=F32)
    fgt = fg.T[:N_ATTN_HEADS, :] + bfg_ref[:, 0:1]
    ls_ref[...] = jnp.minimum(fgt, 0.0) - jnp.log1p(jnp.exp(-jnp.abs(fgt)))

    lbl = lbl_ref[...]
    e = jnp.exp(lbl - jnp.max(lbl, axis=0, keepdims=True))
    lb = e[0:1, :] / jnp.sum(e, axis=0, keepdims=True)
    o0 = 3 * D_ATTN
    qh = p[:, o0:o0 + D_HGRN]
    z = p[:, o0 + D_HGRN:o0 + 2 * D_HGRN]
    ih = p[:, o0 + 2 * D_HGRN:o0 + 3 * D_HGRN]
    gh = p[:, o0 + 3 * D_HGRN:o0 + 4 * D_HGRN]
    t = jnp.exp(-jnp.abs(z))
    r = 1.0 / (1.0 + t)
    tr = t * r
    pos = z >= 0.0
    sig_p = jnp.where(pos, r, tr)
    sig_n = jnp.where(pos, tr, r)
    lf_ref[...] = jnp.log(lb + (1.0 - lb) * sig_p)
    kk_ref[...] = ((1.0 - lb) * sig_n).astype(BF16)
    qf_ref[...] = _silu(qh).astype(BF16)
    iv_ref[...] = ih.astype(BF16)
    gt_ref[...] = _silu(gh).astype(BF16)


def _mix_in(x2d, mod, g, w_main, w_fg, b_fg, q_g2, k_g2, lb_logits):
    t = x2d.shape[0]
    tm = TM_MIX
    tiles_per_seq = SEQ // tm
    row_blk = lambda n: pl.BlockSpec((tm, n), lambda i: (i, 0))
    bf = lambda n: jax.ShapeDtypeStruct((t, n), BF16)
    return pl.pallas_call(
        _mix_in_kernel,
        grid=(t // tm,),
        in_specs=[
            row_blk(D_MODEL),
            pl.BlockSpec((1, N_MOD, D_MODEL), lambda i: (i // tiles_per_seq, 0, 0)),
            _resident((1, D_MODEL)),
            _resident((D_MODEL, N_MAIN)),
            _resident((D_MODEL, LANES)),
            _resident((N_ATTN_HEADS, LANES)),
            _resident((1, LANES)),
            _resident((1, LANES)),
            _resident((2, D_HGRN)),
        ],
        out_specs=[
            row_blk(D_ATTN), row_blk(D_ATTN), row_blk(D_ATTN),
            pl.BlockSpec((None, N_ATTN_HEADS, tm),
                         lambda i: (i // tiles_per_seq, 0, i % tiles_per_seq)),
            row_blk(D_HGRN), row_blk(D_HGRN), row_blk(D_HGRN), row_blk(D_HGRN), row_blk(D_HGRN),
        ],
        out_shape=[
            bf(D_ATTN), bf(D_ATTN), bf(D_ATTN),
            jax.ShapeDtypeStruct((BATCH, N_ATTN_HEADS, SEQ), F32),
            bf(D_HGRN), jax.ShapeDtypeStruct((t, D_HGRN), F32), bf(D_HGRN), bf(D_HGRN), bf(D_HGRN),
        ],
        compiler_params=pltpu.CompilerParams(
            dimension_semantics=("arbitrary",), vmem_limit_bytes=VMEM_LIMIT),
        name="mix_in",
    )(x2d, mod, g, w_main, w_fg, b_fg, q_g2, k_g2, lb_logits)


def _fcum_kernel(ls_ref, nf_ref):
    blk = 256
    r = lax.broadcasted_iota(jnp.int32, (blk, blk), 0)
    c = lax.broadcasted_iota(jnp.int32, (blk, blk), 1)
    upper = (r <= c).astype(F32)
    carry = jnp.zeros((N_ATTN_HEADS, 1), F32)
    for j in range(SEQ // blk):
        x = ls_ref[0, :, j * blk:(j + 1) * blk]
        cs = jnp.dot(x, upper, precision=lax.Precision.HIGHEST,
                     preferred_element_type=F32) + carry
        nf_ref[0, :, j * blk:(j + 1) * blk] = -cs
        carry = cs[:, blk - 1:blk]


def _fcum(ls):
    spec = pl.BlockSpec((1, N_ATTN_HEADS, SEQ), lambda b: (b, 0, 0))
    return pl.pallas_call(
        _fcum_kernel,
        grid=(BATCH,),
        in_specs=[spec],
        out_specs=spec,
        out_shape=jax.ShapeDtypeStruct((BATCH, N_ATTN_HEADS, SEQ), F32),
        compiler_params=pltpu.CompilerParams(dimension_semantics=("arbitrary",)),
        name="fcum",
    )(ls)


def _attn_kernel(q_ref, k_ref, v_ref, nf_ref, o_ref, m_ref, l_ref, acc_ref):
    hp = pl.program_id(1)
    lo = lax.broadcasted_iota(jnp.int32, (1, LANES), 1) < ATTN_HEAD_DIM
    r_i = lax.broadcasted_iota(jnp.int32, (TQ, TK), 0)
    c_i = lax.broadcasted_iota(jnp.int32, (TQ, TK), 1)
    causal = c_i <= r_i
    nt = (((1,), (1,)), ((), ()))

    def q_block(qi, _):
        q0 = pl.multiple_of(qi * TQ, TQ)
        q = q_ref[0, pl.ds(q0, TQ), :]
        outs = []
        for a in range(2):
            qa = jnp.where(lo if a == 0 else jnp.logical_not(lo), q, jnp.zeros_like(q))
            head = 2 * hp + a
            m_ref[...] = jnp.full(m_ref.shape, -jnp.inf, F32)
            l_ref[...] = jnp.zeros(l_ref.shape, F32)
            acc_ref[...] = jnp.zeros(acc_ref.shape, F32)

            def step(kj, masked):
                k0 = pl.multiple_of(kj * TK, TK)
                k = k_ref[0, pl.ds(k0, TK), :]
                v = v_ref[0, pl.ds(k0, TK), :]
                s = lax.dot_general(qa, k, nt, preferred_element_type=F32)
                s = s + nf_ref[0, head, pl.ds(kj, 1), :]
                if masked:
                    s = jnp.where(causal, s, -jnp.inf)
                m_old = m_ref[...]
                m_new = jnp.maximum(m_old, jnp.max(s, axis=-1, keepdims=True))
                alpha = jnp.exp(m_old - m_new)
                p = jnp.exp(s - m_new)
                l_ref[...] = alpha * l_ref[...] + jnp.sum(p, axis=-1, keepdims=True)
                acc_ref[...] = alpha * acc_ref[...] + jnp.dot(
                    p.astype(BF16), v, preferred_element_type=F32)
                m_ref[...] = m_new

            def full_step(kj, c):
                step(kj, False)
                return c

            lax.fori_loop(0, qi, full_step, 0)
            step(qi, True)
            outs.append(acc_ref[...] / l_ref[...])
        o_ref[0, pl.ds(q0, TQ), :] = jnp.where(lo, outs[0], outs[1]).astype(o_ref.dtype)
        return 0

    lax.fori_loop(0, SEQ // TQ, q_block, 0)


def _attn(qa, ka, va, nf):
    qkv_spec = pl.BlockSpec((1, SEQ, LANES), lambda b, h: (b, 0, h))
    return pl.pallas_call(
        _attn_kernel,
        grid=(BATCH, D_ATTN // LANES),
        in_specs=[
            qkv_spec, qkv_spec, qkv_spec,
            pl.BlockSpec((1, N_ATTN_HEADS, SEQ // TK, TK), lambda b, h: (b, 0, 0, 0)),
        ],
        out_specs=qkv_spec,
        out_shape=jax.ShapeDtypeStruct((BATCH, SEQ, D_ATTN), BF16),
        scratch_shapes=[
            pltpu.VMEM((TQ, 1), F32),
            pltpu.VMEM((TQ, 1), F32),
            pltpu.VMEM((TQ, LANES), F32),
        ],
        compiler_params=pltpu.CompilerParams(
            dimension_semantics=("arbitrary", "arbitrary"), vmem_limit_bytes=VMEM_LIMIT),
        name="attn",
    )(qa, ka, va, nf)


def _hgrn_kernel(qf_ref, kk_ref, iv_ref, lf_ref, gt_ref, g_ref, o_ref, st_ref):
    c_len = HGRN_C
    n_chunks = SEQ // c_len
    n8 = c_len // SUBLANES
    st_ref[...] = jnp.zeros(st_ref.shape, F32)

    row = lax.broadcasted_iota(jnp.int32, (c_len, LANES), 0)
    r_i = lax.broadcasted_iota(jnp.int32, (c_len, c_len), 0)
    c_i = lax.broadcasted_iota(jnp.int32, (c_len, c_len), 1)
    halves = []
    hh = SUBLANES
    while hh < c_len:
        halves.append(hh)
        hh *= 2
    lvl_masks = [
        ((r_i // (2 * h)) == (c_i // (2 * h))) & ((r_i % (2 * h)) >= h) & ((c_i % (2 * h)) < h)
        for h in halves
    ]
    diag_masks = [
        (c_i == (r_i // SUBLANES) * SUBLANES + j) & ((r_i % SUBLANES) >= j)
        for j in range(SUBLANES)
    ]
    nt = (((1,), (1,)), ((), ()))
    tn = (((0,), (0,)), ((), ()))

    def rows_bcast(x, group, idx):
        if group == c_len:
            return jnp.broadcast_to(x[idx:idx + 1, :], x.shape)
        x3 = x.reshape(c_len // group, group, LANES)
        return jnp.broadcast_to(x3[:, idx:idx + 1, :], x3.shape).reshape(c_len, LANES)

    def chunk(ci, _):
        t0 = pl.multiple_of(ci * c_len, c_len)
        for hd in range(N_HGRN_HEADS):
            sl = slice(hd * HGRN_K, (hd + 1) * HGRN_K)
            lf = lf_ref[0, pl.ds(t0, c_len), sl]
            qf = qf_ref[0, pl.ds(t0, c_len), sl].astype(F32)
            kk = kk_ref[0, pl.ds(t0, c_len), sl].astype(F32)
            iv = iv_ref[0, pl.ds(t0, c_len), sl]
            b = lf
            k = 1
            while k < c_len:
                b = b + jnp.where(row >= k, pltpu.roll(b, k, 0), 0.0)
                k *= 2
            b_last = b[c_len - 1:c_len, :]
            st = st_ref[hd]
            qi = (qf * jnp.exp(b)).astype(BF16)
            o = lax.dot_general(qi, st.astype(BF16), nt, preferred_element_type=F32)
            a = jnp.zeros((c_len, c_len), F32)
            for h, msk in zip(halves, lvl_masks):
                ref = rows_bcast(b, 2 * h, h - 1)
                ql = (qf * jnp.exp(jnp.minimum(b - ref, 0.0))).astype(BF16)
                kl = (kk * jnp.exp(jnp.minimum(ref - b, 0.0))).astype(BF16)
                al = lax.dot_general(ql, kl, nt, preferred_element_type=F32)
                a = jnp.where(msk, al, a)
            for j in range(SUBLANES):
                bs = rows_bcast(b, SUBLANES, j)
                ks = rows_bcast(kk, SUBLANES, j)
                e = jnp.exp(jnp.minimum(b - bs, 0.0))
                col = jnp.sum(qf * ks * e, axis=-1, keepdims=True)
                a = jnp.where(diag_masks[j], col, a)
            o = o + jnp.dot(a.astype(BF16), iv, preferred_element_type=F32)
            kd = (kk * jnp.exp(b_last - b)).astype(BF16)
            st_ref[hd] = st * jnp.exp(b_last) + lax.dot_general(
                iv, kd, tn, preferred_element_type=F32)
            ms = jnp.mean(o * o, axis=-1, keepdims=True)
            y = o * lax.rsqrt(ms + EPS) * g_ref[:, sl]
            y = y * gt_ref[0, pl.ds(t0, c_len), sl].astype(F32)
            o_ref[0, pl.ds(t0, c_len), sl] = y.astype(o_ref.dtype)
        return 0

    lax.fori_loop(0, n_chunks, chunk, 0)


def _hgrn(qf, kk, iv, lf, gt, g):
    spec = pl.BlockSpec((1, SEQ, D_HGRN), lambda b: (b, 0, 0))
    return pl.pallas_call(
        _hgrn_kernel,
        grid=(BATCH,),
        in_specs=[spec, spec, spec, spec, spec,
                  pl.BlockSpec((1, D_HGRN), lambda b: (0, 0))],
        out_specs=spec,
        out_shape=jax.ShapeDtypeStruct((BATCH, SEQ, D_HGRN), BF16),
        scratch_shapes=[pltpu.VMEM((N_HGRN_HEADS, HGRN_K, HGRN_K), F32)],
        compiler_params=pltpu.CompilerParams(
            dimension_semantics=("arbitrary",), vmem_limit_bytes=VMEM_LIMIT),
        name="hgrn",
    )(qf, kk, iv, lf, gt, g)


def _mix_out_kernel(x_ref, oa_ref, oh_ref, mod_ref, ag_ref, w_ref, o_ref):
    oa = oa_ref[...].astype(F32)
    ms = jnp.mean(oa * oa, axis=-1, keepdims=True)
    oa = (oa * lax.rsqrt(ms + EPS) * ag_ref[...]).astype(BF16)
    cat = jnp.concatenate([oa, oh_ref[...]], axis=-1)
    out = jnp.dot(cat, w_ref[...], preferred_element_type=F32)
    o_ref[...] = x_ref[...] + mod_ref[0, 5:6, :] * out


def _mix_out(x2d, oa, oh, mod, ag, w):
    t = x2d.shape[0]
    tm = TM_MIX
    tiles_per_seq = SEQ // tm
    return pl.pallas_call(
        _mix_out_kernel,
        grid=(t // tm,),
        in_specs=[
            pl.BlockSpec((tm, D_MODEL), lambda i: (i, 0)),
            pl.BlockSpec((tm, D_ATTN), lambda i: (i, 0)),
            pl.BlockSpec((tm, D_HGRN), lambda i: (i, 0)),
            pl.BlockSpec((1, N_MOD, D_MODEL), lambda i: (i // tiles_per_seq, 0, 0)),
            _resident((1, D_ATTN)),
            _resident((D_MODEL, D_MODEL)),
        ],
        out_specs=pl.BlockSpec((tm, D_MODEL), lambda i: (i, 0)),
        out_shape=jax.ShapeDtypeStruct((t, D_MODEL), F32),
        compiler_params=pltpu.CompilerParams(
            dimension_semantics=("arbitrary",), vmem_limit_bytes=VMEM_LIMIT),
        name="mix_out",
    )(x2d, oa, oh, mod, ag, w)


def kernel(x, c, w_ada, b_ada, g_norm1, ffn1_w_in, ffn1_w_out, g_norm_mix, w_in_mix, b_fgate,
           q_norm_g, k_norm_g, attn_out_g, hgrn_lb_logits, hgrn_out_g, w_out_mix, g_norm2,
           ffn2_w_in, ffn2_w_out):
    assert x.shape == (BATCH, SEQ, D_MODEL) and w_ada.shape[0] == 1
    t = BATCH * SEQ
    mod = _adaln(c, w_ada[0], b_ada).reshape(BATCH, N_MOD, D_MODEL)

    x0 = x.reshape(t, D_MODEL)
    x1 = _ffn(x0, mod, g_norm1, ffn1_w_in[0].astype(BF16), ffn1_w_out[0].astype(BF16), 0, "ffn1")

    w_mix = w_in_mix[0]
    fg0 = 3 * D_ATTN
    w_main = jnp.concatenate([w_mix[:, :fg0], w_mix[:, fg0 + N_ATTN_HEADS:]], axis=1).astype(BF16)
    w_fg = jnp.pad(w_mix[:, fg0:fg0 + N_ATTN_HEADS], ((0, 0), (0, LANES - N_ATTN_HEADS))).astype(BF16)
    b_fg = jnp.broadcast_to(b_fgate[0][:, None], (N_ATTN_HEADS, LANES))
    q_g2 = jnp.tile(q_norm_g, (1, LANES // ATTN_HEAD_DIM))
    k_g2 = jnp.tile(k_norm_g, (1, LANES // ATTN_HEAD_DIM))
    qa, ka, va, ls, qf, lf, kk, iv, gt = _mix_in(
        x1, mod, g_norm_mix, w_main, w_fg, b_fg, q_g2, k_g2, hgrn_lb_logits)

    nf = _fcum(ls).reshape(BATCH, N_ATTN_HEADS, SEQ // TK, TK)
    seq3 = lambda a: a.reshape(BATCH, SEQ, a.shape[-1])
    oa = _attn(seq3(qa), seq3(ka), seq3(va), nf)
    oh = _hgrn(seq3(qf), seq3(kk), seq3(iv), seq3(lf), seq3(gt), hgrn_out_g)

    x2 = _mix_out(x1, oa.reshape(t, D_ATTN), oh.reshape(t, D_HGRN), mod, attn_out_g,
                  w_out_mix[0].astype(BF16))
    x3 = _ffn(x2, mod, g_norm2, ffn2_w_in[0].astype(BF16), ffn2_w_out[0].astype(BF16), 6, "ffn2")
    return x3.reshape(BATCH, SEQ, D_MODEL)
```

```python
import functools

import jax
import jax.numpy as jnp
from jax import lax
from jax.experimental import pallas as pl
from jax.experimental.pallas import tpu as pltpu

D_MODEL = 1024
BATCH = 32
SEQ = 2048
D_ATTN = 512
D_HGRN = 512
ATTN_HEAD_DIM = 64
N_ATTN_HEADS = 8
HGRN_K = 128
N_HGRN_HEADS = 4
D_FF = 2816
N_MOD = 9
FFN_RES = 0.5
EPS = 1e-6

LANES = 128
SUBLANES = 8
N_MAIN = 3 * D_ATTN + 4 * D_HGRN

F32 = jnp.float32
BF16 = jnp.bfloat16

TM_FFN = 512
TM_MIX = 512
TQ = 256
HGRN_C = 64
VMEM_LIMIT = 56 * 1024 * 1024


def _sigmoid(x):
    return 1.0 / (1.0 + jnp.exp(-x))


def _silu(x):
    return x * _sigmoid(x)


def _modulated_rms(x, g, shift, scale):
    ms = jnp.mean(x * x, axis=-1, keepdims=True)
    y = x * lax.rsqrt(ms + EPS) * g
    return y * (1.0 + scale) + shift


def _adaln_kernel(c_ref, w_ref, b_ref, o_ref):
    cs = _silu(c_ref[...]).astype(BF16)
    w = w_ref[...].astype(BF16)
    o_ref[...] = jnp.dot(cs, w, preferred_element_type=F32) + b_ref[...]


def _adaln(c, w, b):
    n = w.shape[1]
    tn = D_MODEL
    return pl.pallas_call(
        _adaln_kernel,
        grid=(n // tn,),
        in_specs=[
            pl.BlockSpec((BATCH, D_MODEL), lambda j: (0, 0)),
            pl.BlockSpec((D_MODEL, tn), lambda j: (0, j)),
            pl.BlockSpec((1, tn), lambda j: (0, j)),
        ],
        out_specs=pl.BlockSpec((BATCH, tn), lambda j: (0, j)),
        out_shape=jax.ShapeDtypeStruct((BATCH, n), F32),
        compiler_params=pltpu.CompilerParams(dimension_semantics=("arbitrary",)),
        name="adaln",
    )(c, w, b)


def _ffn_kernel(*refs, mod_base, with_mixer_out):
    if with_mixer_out:
        x_ref, mod_ref, g_ref, win_ref, wout_ref, oa_ref, oh_ref, ag_ref, wmo_ref, o_ref = refs
        oa = oa_ref[...].astype(F32)
        ms = jnp.mean(oa * oa, axis=-1, keepdims=True)
        oa = (oa * lax.rsqrt(ms + EPS) * ag_ref[...]).astype(BF16)
        cat = jnp.concatenate([oa, oh_ref[...]], axis=-1)
        mixed = jnp.dot(cat, wmo_ref[...], preferred_element_type=F32)
        x = x_ref[...] + mod_ref[0, 5:6, :] * mixed
    else:
        x_ref, mod_ref, g_ref, win_ref, wout_ref, o_ref = refs
        x = x_ref[...]
    shift = mod_ref[0, mod_base:mod_base + 1, :]
    scale = mod_ref[0, mod_base + 1:mod_base + 2, :]
    gate = mod_ref[0, mod_base + 2:mod_base + 3, :]
    h = _modulated_rms(x, g_ref[...], shift, scale).astype(BF16)
    gu = jnp.dot(h, win_ref[...], preferred_element_type=F32)
    a = (_silu(gu[:, :D_FF]) * gu[:, D_FF:]).astype(BF16)
    out = jnp.dot(a, wout_ref[...], preferred_element_type=F32)
    o_ref[...] = x + (FFN_RES * gate) * out


def _resident(shape):
    return pl.BlockSpec(shape, lambda i: (0,) * len(shape), pipeline_mode=pl.Buffered(1))


def _ffn(x2d, mod, g, w_in, w_out, mod_base, name, mixer_out=None):
    t = x2d.shape[0]
    tiles_per_seq = SEQ // TM_FFN
    row_blk = lambda n: pl.BlockSpec((TM_FFN, n), lambda i: (i, 0))
    in_specs = [
        row_blk(D_MODEL),
        pl.BlockSpec((1, N_MOD, D_MODEL), lambda i: (i // tiles_per_seq, 0, 0)),
        _resident((1, D_MODEL)),
        _resident((D_MODEL, 2 * D_FF)),
        _resident((D_FF, D_MODEL)),
    ]
    args = [x2d, mod, g, w_in, w_out]
    if mixer_out is not None:
        in_specs += [row_blk(D_ATTN), row_blk(D_HGRN), _resident((1, D_ATTN)),
                     _resident((D_MODEL, D_MODEL))]
        args += list(mixer_out)
    return pl.pallas_call(
        functools.partial(_ffn_kernel, mod_base=mod_base, with_mixer_out=mixer_out is not None),
        grid=(t // TM_FFN,),
        in_specs=in_specs,
        out_specs=row_blk(D_MODEL),
        out_shape=jax.ShapeDtypeStruct((t, D_MODEL), F32),
        compiler_params=pltpu.CompilerParams(
            dimension_semantics=("arbitrary",), vmem_limit_bytes=VMEM_LIMIT),
        name=name,
    )(*args)


def _mix_in_kernel(x_ref, mod_ref, g_ref, w_ref, wfg_ref, bfg_ref, qg_ref, kg_ref, lbl_ref,
                   qa_ref, ka_ref, va_ref, ls_ref, qf_ref, lf_ref, kk_ref, iv_ref, gt_ref):
    x = x_ref[...]
    h = _modulated_rms(x, g_ref[...], mod_ref[0, 3:4, :], mod_ref[0, 4:5, :]).astype(BF16)
    p = jnp.dot(h, w_ref[...], preferred_element_type=F32)

    lo = lax.broadcasted_iota(jnp.int32, (1, LANES), 1) < ATTN_HEAD_DIM
    inv_dh = 1.0 / ATTN_HEAD_DIM

    def head_norm(t, gain):
        sq = t * t
        s_lo = jnp.sum(jnp.where(lo, sq, 0.0), axis=-1, keepdims=True)
        s_hi = jnp.sum(jnp.where(lo, 0.0, sq), axis=-1, keepdims=True)
        r = jnp.where(lo, lax.rsqrt(s_lo * inv_dh + EPS), lax.rsqrt(s_hi * inv_dh + EPS))
        return (t * r * gain).astype(BF16)

    q_gain = qg_ref[...] * (ATTN_HEAD_DIM ** -0.5)
    k_gain = kg_ref[...]
    for j in range(D_ATTN // LANES):
        sl = slice(j * LANES, (j + 1) * LANES)
        qa_ref[:, sl] = head_norm(p[:, j * LANES:(j + 1) * LANES], q_gain)
        ka_ref[:, sl] = head_norm(p[:, D_ATTN + j * LANES:D_ATTN + (j + 1) * LANES], k_gain)
    va_ref[...] = p[:, 2 * D_ATTN:3 * D_ATTN].astype(BF16)

    fg = jnp.dot(h, wfg_ref[...], preferred_element_type=F32)
    fgt = fg.T[:N_ATTN_HEADS, :] + bfg_ref[:, 0:1]
    ls_ref[...] = jnp.minimum(fgt, 0.0) - jnp.log1p(jnp.exp(-jnp.abs(fgt)))

    lbl = lbl_ref[...]
    e = jnp.exp(lbl - jnp.max(lbl, axis=0, keepdims=True))
    lb = e[0:1, :] / jnp.sum(e, axis=0, keepdims=True)
    o0 = 3 * D_ATTN
    qh = p[:, o0:o0 + D_HGRN]
    z = p[:, o0 + D_HGRN:o0 + 2 * D_HGRN]
    ih = p[:, o0 + 2 * D_HGRN:o0 + 3 * D_HGRN]
    gh = p[:, o0 + 3 * D_HGRN:o0 + 4 * D_HGRN]
    t = jnp.exp(-jnp.abs(z))
    r = 1.0 / (1.0 + t)
    tr = t * r
    pos = z >= 0.0
    sig_p = jnp.where(pos, r, tr)
    sig_n = jnp.where(pos, tr, r)
    lf_ref[...] = jnp.log(lb + (1.0 - lb) * sig_p)
    kk_ref[...] = ((1.0 - lb) * sig_n).astype(BF16)
    qf_ref[...] = _silu(qh).astype(BF16)
    iv_ref[...] = ih.astype(BF16)
    gt_ref[...] = _silu(gh).astype(BF16)


def _mix_in(x2d, mod, g, w_main, w_fg, b_fg, q_g2, k_g2, lb_logits):
    t = x2d.shape[0]
    tm = TM_MIX
    tiles_per_seq = SEQ // tm
    row_blk = lambda n: pl.BlockSpec((tm, n), lambda i: (i, 0))
    bf = lambda n: jax.ShapeDtypeStruct((t, n), BF16)
    return pl.pallas_call(
        _mix_in_kernel,
        grid=(t // tm,),
        in_specs=[
            row_blk(D_MODEL),
            pl.BlockSpec((1, N_MOD, D_MODEL), lambda i: (i // tiles_per_seq, 0, 0)),
            _resident((1, D_MODEL)),
            _resident((D_MODEL, N_MAIN)),
            _resident((D_MODEL, LANES)),
            _resident((N_ATTN_HEADS, LANES)),
            _resident((1, LANES)),
            _resident((1, LANES)),
            _resident((2, D_HGRN)),
        ],
        out_specs=[
            row_blk(D_ATTN), row_blk(D_ATTN), row_blk(D_ATTN),
            pl.BlockSpec((None, N_ATTN_HEADS, tm),
                         lambda i: (i // tiles_per_seq, 0, i % tiles_per_seq)),
            row_blk(D_HGRN), row_blk(D_HGRN), row_blk(D_HGRN), row_blk(D_HGRN), row_blk(D_HGRN),
        ],
        out_shape=[
            bf(D_ATTN), bf(D_ATTN), bf(D_ATTN),
            jax.ShapeDtypeStruct((BATCH, N_ATTN_HEADS, SEQ), F32),
            bf(D_HGRN), jax.ShapeDtypeStruct((t, D_HGRN), F32), bf(D_HGRN), bf(D_HGRN), bf(D_HGRN),
        ],
        compiler_params=pltpu.CompilerParams(
            dimension_semantics=("arbitrary",), vmem_limit_bytes=VMEM_LIMIT),
        name="mix_in",
    )(x2d, mod, g, w_main, w_fg, b_fg, q_g2, k_g2, lb_logits)


def _fcum_kernel(ls_ref, nf_ref):
    blk = 256
    r = lax.broadcasted_iota(jnp.int32, (blk, blk), 0)
    c = lax.broadcasted_iota(jnp.int32, (blk, blk), 1)
    upper = (r <= c).astype(F32)
    carry = jnp.zeros((N_ATTN_HEADS, 1), F32)
    for j in range(SEQ // blk):
        x = ls_ref[0, :, j * blk:(j + 1) * blk]
        cs = jnp.dot(x, upper, precision=lax.Precision.HIGHEST,
                     preferred_element_type=F32) + carry
        nf_ref[0, :, j * blk:(j + 1) * blk] = -cs
        carry = cs[:, blk - 1:blk]


def _fcum(ls):
    spec = pl.BlockSpec((1, N_ATTN_HEADS, SEQ), lambda b: (b, 0, 0))
    return pl.pallas_call(
        _fcum_kernel,
        grid=(BATCH,),
        in_specs=[spec],
        out_specs=spec,
        out_shape=jax.ShapeDtypeStruct((BATCH, N_ATTN_HEADS, SEQ), F32),
        compiler_params=pltpu.CompilerParams(dimension_semantics=("arbitrary",)),
        name="fcum",
    )(ls)


def _attn_kernel(q_ref, k_ref, v_ref, nf_ref, o_ref):
    hp = pl.program_id(1)
    lo = lax.broadcasted_iota(jnp.int32, (1, LANES), 1) < ATTN_HEAD_DIM
    r_i = lax.broadcasted_iota(jnp.int32, (TQ, TQ), 0)
    c_i = lax.broadcasted_iota(jnp.int32, (TQ, TQ), 1)
    causal = c_i <= r_i
    nt = (((1,), (1,)), ((), ()))

    for qi in range(SEQ // TQ):
        q0 = qi * TQ
        q = q_ref[0, q0:q0 + TQ, :]
        k_d = k_ref[0, q0:q0 + TQ, :]
        v_d = v_ref[0, q0:q0 + TQ, :]
        outs = []
        for a in range(2):
            qa = jnp.where(lo if a == 0 else jnp.logical_not(lo), q, jnp.zeros_like(q))
            nf = nf_ref[0, pl.ds(2 * hp + a, 1), :]
            s_d = lax.dot_general(qa, k_d, nt, preferred_element_type=F32) + nf[:, q0:q0 + TQ]
            s_d = jnp.where(causal, s_d, -jnp.inf)
            m = jnp.max(s_d, axis=-1, keepdims=True)
            if qi > 0:
                s_f = lax.dot_general(qa, k_ref[0, :q0, :], nt,
                                      preferred_element_type=F32) + nf[:, :q0]
                m = jnp.maximum(m, jnp.max(s_f, axis=-1, keepdims=True))
                p_f = jnp.exp(s_f - m)
                l = jnp.sum(p_f, axis=-1, keepdims=True)
                acc = jnp.dot(p_f.astype(BF16), v_ref[0, :q0, :], preferred_element_type=F32)
            p_d = jnp.exp(s_d - m)
            l_d = jnp.sum(p_d, axis=-1, keepdims=True)
            acc_d = jnp.dot(p_d.astype(BF16), v_d, preferred_element_type=F32)
            if qi > 0:
                l = l + l_d
                acc = acc + acc_d
            else:
                l, acc = l_d, acc_d
            outs.append(acc * (1.0 / l))
        o_ref[0, q0:q0 + TQ, :] = jnp.where(lo, outs[0], outs[1]).astype(o_ref.dtype)


def _attn(qa, ka, va, nf):
    qkv_spec = pl.BlockSpec((1, SEQ, LANES), lambda b, h: (b, 0, h))
    return pl.pallas_call(
        _attn_kernel,
        grid=(BATCH, D_ATTN // LANES),
        in_specs=[
            qkv_spec, qkv_spec, qkv_spec,
            pl.BlockSpec((1, N_ATTN_HEADS, SEQ), lambda b, h: (b, 0, 0)),
        ],
        out_specs=qkv_spec,
        out_shape=jax.ShapeDtypeStruct((BATCH, SEQ, D_ATTN), BF16),
        compiler_params=pltpu.CompilerParams(
            dimension_semantics=("arbitrary", "arbitrary"), vmem_limit_bytes=VMEM_LIMIT),
        name="attn",
    )(qa, ka, va, nf)


def _hgrn_kernel(qf_ref, kk_ref, iv_ref, lf_ref, gt_ref, g_ref, o_ref, st_ref):
    c_len = HGRN_C
    n_chunks = SEQ // c_len
    n8 = c_len // SUBLANES
    st_ref[...] = jnp.zeros(st_ref.shape, F32)

    row = lax.broadcasted_iota(jnp.int32, (c_len, LANES), 0)
    r_i = lax.broadcasted_iota(jnp.int32, (c_len, c_len), 0)
    c_i = lax.broadcasted_iota(jnp.int32, (c_len, c_len), 1)
    halves = []
    hh = SUBLANES
    while hh < c_len:
        halves.append(hh)
        hh *= 2
    lvl_masks = [
        ((r_i // (2 * h)) == (c_i // (2 * h))) & ((r_i % (2 * h)) >= h) & ((c_i % (2 * h)) < h)
        for h in halves
    ]
    diag_masks = [
        (c_i == (r_i // SUBLANES) * SUBLANES + j) & ((r_i % SUBLANES) >= j)
        for j in range(SUBLANES)
    ]
    nt = (((1,), (1,)), ((), ()))
    tn = (((0,), (0,)), ((), ()))

    def rows_bcast(x, group, idx):
        if group == c_len:
            return jnp.broadcast_to(x[idx:idx + 1, :], x.shape)
        x3 = x.reshape(c_len // group, group, LANES)
        return jnp.broadcast_to(x3[:, idx:idx + 1, :], x3.shape).reshape(c_len, LANES)

    def chunk(ci, _):
        t0 = pl.multiple_of(ci * c_len, c_len)
        for hd in range(N_HGRN_HEADS):
            sl = slice(hd * HGRN_K, (hd + 1) * HGRN_K)
            lf = lf_ref[0, pl.ds(t0, c_len), sl]
            qf = qf_ref[0, pl.ds(t0, c_len), sl].astype(F32)
            kk = kk_ref[0, pl.ds(t0, c_len), sl].astype(F32)
            iv = iv_ref[0, pl.ds(t0, c_len), sl]
            b = lf
            k = 1
            while k < c_len:
                b = b + jnp.where(row >= k, pltpu.roll(b, k, 0), 0.0)
                k *= 2
            b_last = b[c_len - 1:c_len, :]
            st = st_ref[hd]
            qi = (qf * jnp.exp(b)).astype(BF16)
            o = lax.dot_general(qi, st.astype(BF16), nt, preferred_element_type=F32)
            a = jnp.zeros((c_len, c_len), F32)
            for h, msk in zip(halves, lvl_masks):
                ref = rows_bcast(b, 2 * h, h - 1)
                ql = (qf * jnp.exp(jnp.minimum(b - ref, 0.0))).astype(BF16)
                kl = (kk * jnp.exp(jnp.minimum(ref - b, 0.0))).astype(BF16)
                al = lax.dot_general(ql, kl, nt, preferred_element_type=F32)
                a = jnp.where(msk, al, a)
            for j in range(SUBLANES):
                bs = rows_bcast(b, SUBLANES, j)
                ks = rows_bcast(kk, SUBLANES, j)
                e = jnp.exp(jnp.minimum(b - bs, 0.0))
                col = jnp.sum(qf * ks * e, axis=-1, keepdims=True)
                a = jnp.where(diag_masks[j], col, a)
            o = o + jnp.dot(a.astype(BF16), iv, preferred_element_type=F32)
            kd = (kk * jnp.exp(b_last - b)).astype(BF16)
            st_ref[hd] = st * jnp.exp(b_last) + lax.dot_general(
                iv, kd, tn, preferred_element_type=F32)
            ms = jnp.mean(o * o, axis=-1, keepdims=True)
            y = o * lax.rsqrt(ms + EPS) * g_ref[:, sl]
            y = y * gt_ref[0, pl.ds(t0, c_len), sl].astype(F32)
            o_ref[0, pl.ds(t0, c_len), sl] = y.astype(o_ref.dtype)
        return 0

    lax.fori_loop(0, n_chunks, chunk, 0)


def _hgrn(qf, kk, iv, lf, gt, g):
    spec = pl.BlockSpec((1, SEQ, D_HGRN), lambda b: (b, 0, 0))
    return pl.pallas_call(
        _hgrn_kernel,
        grid=(BATCH,),
        in_specs=[spec, spec, spec, spec, spec,
                  pl.BlockSpec((1, D_HGRN), lambda b: (0, 0))],
        out_specs=spec,
        out_shape=jax.ShapeDtypeStruct((BATCH, SEQ, D_HGRN), BF16),
        scratch_shapes=[pltpu.VMEM((N_HGRN_HEADS, HGRN_K, HGRN_K), F32)],
        compiler_params=pltpu.CompilerParams(
            dimension_semantics=("arbitrary",), vmem_limit_bytes=VMEM_LIMIT),
        name="hgrn",
    )(qf, kk, iv, lf, gt, g)


def kernel(x, c, w_ada, b_ada, g_norm1, ffn1_w_in, ffn1_w_out, g_norm_mix, w_in_mix, b_fgate,
           q_norm_g, k_norm_g, attn_out_g, hgrn_lb_logits, hgrn_out_g, w_out_mix, g_norm2,
           ffn2_w_in, ffn2_w_out):
    assert x.shape == (BATCH, SEQ, D_MODEL) and w_ada.shape[0] == 1
    t = BATCH * SEQ
    mod = _adaln(c, w_ada[0], b_ada).reshape(BATCH, N_MOD, D_MODEL)

    x0 = x.reshape(t, D_MODEL)
    x1 = _ffn(x0, mod, g_norm1, ffn1_w_in[0].astype(BF16), ffn1_w_out[0].astype(BF16), 0, "ffn1")

    w_mix = w_in_mix[0]
    fg0 = 3 * D_ATTN
    w_main = jnp.concatenate([w_mix[:, :fg0], w_mix[:, fg0 + N_ATTN_HEADS:]], axis=1).astype(BF16)
    w_fg = jnp.pad(w_mix[:, fg0:fg0 + N_ATTN_HEADS], ((0, 0), (0, LANES - N_ATTN_HEADS))).astype(BF16)
    b_fg = jnp.broadcast_to(b_fgate[0][:, None], (N_ATTN_HEADS, LANES))
    q_g2 = jnp.tile(q_norm_g, (1, LANES // ATTN_HEAD_DIM))
    k_g2 = jnp.tile(k_norm_g, (1, LANES // ATTN_HEAD_DIM))
    qa, ka, va, ls, qf, lf, kk, iv, gt = _mix_in(
        x1, mod, g_norm_mix, w_main, w_fg, b_fg, q_g2, k_g2, hgrn_lb_logits)

    nf = _fcum(ls)
    seq3 = lambda a: a.reshape(BATCH, SEQ, a.shape[-1])
    oa = _attn(seq3(qa), seq3(ka), seq3(va), nf)
    oh = _hgrn(seq3(qf), seq3(kk), seq3(iv), seq3(lf), seq3(gt), hgrn_out_g)

    mixer_out = (oa.reshape(t, D_ATTN), oh.reshape(t, D_HGRN), attn_out_g, w_out_mix[0].astype(BF16))
    x3 = _ffn(x1, mod, g_norm2, ffn2_w_in[0].astype(BF16), ffn2_w_out[0].astype(BF16), 6, "ffn2",
              mixer_out=mixer_out)
    return x3.reshape(BATCH, SEQ, D_MODEL)
```

```python
import functools

import jax
import jax.numpy as jnp
from jax import lax
from jax.experimental import pallas as pl
from jax.experimental.pallas import tpu as pltpu

D_MODEL = 1024
BATCH = 32
SEQ = 2048
D_ATTN = 512
D_HGRN = 512
ATTN_HEAD_DIM = 64
N_ATTN_HEADS = 8
HGRN_K = 128
N_HGRN_HEADS = 4
D_FF = 2816
N_MOD = 9
FFN_RES = 0.5
EPS = 1e-6
LOG2_E = 1.4426950408889634

LANES = 128
SUBLANES = 8
N_MAIN = 3 * D_ATTN + 4 * D_HGRN

F32 = jnp.float32
BF16 = jnp.bfloat16

TM_FFN = 512
TM_MIX = 512
TQ = 256
HGRN_C = 64
HGRN_FAST_C = 128
HGRN_FAST_UNROLL = 4
HGRN_FAST_MAX_EXPONENT = 70.0
VMEM_LIMIT = 56 * 1024 * 1024


def _sigmoid(x):
    return 1.0 / (1.0 + jnp.exp(-x))


def _silu(x):
    return x * _sigmoid(x)


def _modulated_rms(x, g, shift, scale):
    ms = jnp.mean(x * x, axis=-1, keepdims=True)
    y = x * lax.rsqrt(ms + EPS) * g
    return y * (1.0 + scale) + shift


def _adaln_kernel(c_ref, w_ref, b_ref, o_ref):
    cs = _silu(c_ref[...]).astype(BF16)
    w = w_ref[...].astype(BF16)
    o_ref[...] = jnp.dot(cs, w, preferred_element_type=F32) + b_ref[...]


def _adaln(c, w, b):
    n = w.shape[1]
    tn = D_MODEL
    return pl.pallas_call(
        _adaln_kernel,
        grid=(n // tn,),
        in_specs=[
            pl.BlockSpec((BATCH, D_MODEL), lambda j: (0, 0)),
            pl.BlockSpec((D_MODEL, tn), lambda j: (0, j)),
            pl.BlockSpec((1, tn), lambda j: (0, j)),
        ],
        out_specs=pl.BlockSpec((BATCH, tn), lambda j: (0, j)),
        out_shape=jax.ShapeDtypeStruct((BATCH, n), F32),
        compiler_params=pltpu.CompilerParams(dimension_semantics=("arbitrary",)),
        name="adaln",
    )(c, w, b)


def _ffn_kernel(*refs, mod_base, with_mixer_out):
    if with_mixer_out:
        x_ref, mod_ref, g_ref, win_ref, wout_ref, oa_ref, oh_ref, ag_ref, wmo_ref, o_ref = refs
        oa = oa_ref[...].astype(F32)
        ms = jnp.mean(oa * oa, axis=-1, keepdims=True)
        oa = (oa * lax.rsqrt(ms + EPS) * ag_ref[...]).astype(BF16)
        cat = jnp.concatenate([oa, oh_ref[...]], axis=-1)
        mixed = jnp.dot(cat, wmo_ref[...], preferred_element_type=F32)
        x = x_ref[...] + mod_ref[0, 5:6, :] * mixed
    else:
        x_ref, mod_ref, g_ref, win_ref, wout_ref, o_ref = refs
        x = x_ref[...]
    shift = mod_ref[0, mod_base:mod_base + 1, :]
    scale = mod_ref[0, mod_base + 1:mod_base + 2, :]
    gate = mod_ref[0, mod_base + 2:mod_base + 3, :]
    h = _modulated_rms(x, g_ref[...], shift, scale).astype(BF16)
    gu = jnp.dot(h, win_ref[...], preferred_element_type=F32)
    a = (_silu(gu[:, :D_FF]) * gu[:, D_FF:]).astype(BF16)
    out = jnp.dot(a, wout_ref[...], preferred_element_type=F32)
    o_ref[...] = x + (FFN_RES * gate) * out


def _resident(shape):
    return pl.BlockSpec(shape, lambda i: (0,) * len(shape), pipeline_mode=pl.Buffered(1))


def _ffn(x2d, mod, g, w_in, w_out, mod_base, name, mixer_out=None):
    t = x2d.shape[0]
    tiles_per_seq = SEQ // TM_FFN
    row_blk = lambda n: pl.BlockSpec((TM_FFN, n), lambda i: (i, 0))
    in_specs = [
        row_blk(D_MODEL),
        pl.BlockSpec((1, N_MOD, D_MODEL), lambda i: (i // tiles_per_seq, 0, 0)),
        _resident((1, D_MODEL)),
        _resident((D_MODEL, 2 * D_FF)),
        _resident((D_FF, D_MODEL)),
    ]
    args = [x2d, mod, g, w_in, w_out]
    if mixer_out is not None:
        in_specs += [row_blk(D_ATTN), row_blk(D_HGRN), _resident((1, D_ATTN)),
                     _resident((D_MODEL, D_MODEL))]
        args += list(mixer_out)
    return pl.pallas_call(
        functools.partial(_ffn_kernel, mod_base=mod_base, with_mixer_out=mixer_out is not None),
        grid=(t // TM_FFN,),
        in_specs=in_specs,
        out_specs=row_blk(D_MODEL),
        out_shape=jax.ShapeDtypeStruct((t, D_MODEL), F32),
        compiler_params=pltpu.CompilerParams(
            dimension_semantics=("arbitrary",), vmem_limit_bytes=VMEM_LIMIT),
        name=name,
    )(*args)


def _mix_in_kernel(x_ref, mod_ref, g_ref, w_ref, wfg_ref, bfg_ref, qg_ref, kg_ref, lbl_ref,
                   qa_ref, ka_ref, va_ref, ls_ref, qf_ref, lf_ref, kk_ref, iv_ref, gt_ref):
    x = x_ref[...]
    h = _modulated_rms(x, g_ref[...], mod_ref[0, 3:4, :], mod_ref[0, 4:5, :]).astype(BF16)
    p = jnp.dot(h, w_ref[...], preferred_element_type=F32)

    lo = lax.broadcasted_iota(jnp.int32, (1, LANES), 1) < ATTN_HEAD_DIM
    inv_dh = 1.0 / ATTN_HEAD_DIM

    def head_norm(t, gain):
        sq = t * t
        s_lo = jnp.sum(jnp.where(lo, sq, 0.0), axis=-1, keepdims=True)
        s_hi = jnp.sum(jnp.where(lo, 0.0, sq), axis=-1, keepdims=True)
        r = jnp.where(lo, lax.rsqrt(s_lo * inv_dh + EPS), lax.rsqrt(s_hi * inv_dh + EPS))
        return (t * r * gain).astype(BF16)

    q_gain = qg_ref[...] * (ATTN_HEAD_DIM ** -0.5 * LOG2_E)
    k_gain = kg_ref[...]
    for j in range(D_ATTN // LANES):
        sl = slice(j * LANES, (j + 1) * LANES)
        qa_ref[:, sl] = head_norm(p[:, j * LANES:(j + 1) * LANES], q_gain)
        ka_ref[:, sl] = head_norm(p[:, D_ATTN + j * LANES:D_ATTN + (j + 1) * LANES], k_gain)
    va_ref[...] = p[:, 2 * D_ATTN:3 * D_ATTN].astype(BF16)

    fg = jnp.dot(h, wfg_ref[...], preferred_element_type=F32)
    fgt = fg.T[:N_ATTN_HEADS, :] + bfg_ref[:, 0:1]
    ls_ref[...] = jnp.minimum(fgt, 0.0) - jnp.log1p(jnp.exp(-jnp.abs(fgt)))

    lbl = lbl_ref[...]
    e = jnp.exp(lbl - jnp.max(lbl, axis=0, keepdims=True))
    lb = e[0:1, :] / jnp.sum(e, axis=0, keepdims=True)
    o0 = 3 * D_ATTN
    qh = p[:, o0:o0 + D_HGRN]
    z = p[:, o0 + D_HGRN:o0 + 2 * D_HGRN]
    ih = p[:, o0 + 2 * D_HGRN:o0 + 3 * D_HGRN]
    gh = p[:, o0 + 3 * D_HGRN:o0 + 4 * D_HGRN]
    t = jnp.exp(-jnp.abs(z))
    r = 1.0 / (1.0 + t)
    tr = t * r
    pos = z >= 0.0
    sig_p = jnp.where(pos, r, tr)
    sig_n = jnp.where(pos, tr, r)
    lf_ref[...] = jnp.log(lb + (1.0 - lb) * sig_p)
    kk_ref[...] = ((1.0 - lb) * sig_n).astype(BF16)
    qf_ref[...] = _silu(qh).astype(BF16)
    iv_ref[...] = ih.astype(BF16)
    gt_ref[...] = _silu(gh).astype(BF16)


def _mix_in(x2d, mod, g, w_main, w_fg, b_fg, q_g2, k_g2, lb_logits):
    t = x2d.shape[0]
    tm = TM_MIX
    tiles_per_seq = SEQ // tm
    row_blk = lambda n: pl.BlockSpec((tm, n), lambda i: (i, 0))
    bf = lambda n: jax.ShapeDtypeStruct((t, n), BF16)
    return pl.pallas_call(
        _mix_in_kernel,
        grid=(t // tm,),
        in_specs=[
            row_blk(D_MODEL),
            pl.BlockSpec((1, N_MOD, D_MODEL), lambda i: (i // tiles_per_seq, 0, 0)),
            _resident((1, D_MODEL)),
            _resident((D_MODEL, N_MAIN)),
            _resident((D_MODEL, LANES)),
            _resident((N_ATTN_HEADS, LANES)),
            _resident((1, LANES)),
            _resident((1, LANES)),
            _resident((2, D_HGRN)),
        ],
        out_specs=[
            row_blk(D_ATTN), row_blk(D_ATTN), row_blk(D_ATTN),
            pl.BlockSpec((None, N_ATTN_HEADS, tm),
                         lambda i: (i // tiles_per_seq, 0, i % tiles_per_seq)),
            row_blk(D_HGRN), row_blk(D_HGRN), row_blk(D_HGRN), row_blk(D_HGRN), row_blk(D_HGRN),
        ],
        out_shape=[
            bf(D_ATTN), bf(D_ATTN), bf(D_ATTN),
            jax.ShapeDtypeStruct((BATCH, N_ATTN_HEADS, SEQ), F32),
            bf(D_HGRN), jax.ShapeDtypeStruct((t, D_HGRN), F32), bf(D_HGRN), bf(D_HGRN), bf(D_HGRN),
        ],
        compiler_params=pltpu.CompilerParams(
            dimension_semantics=("arbitrary",), vmem_limit_bytes=VMEM_LIMIT),
        name="mix_in",
    )(x2d, mod, g, w_main, w_fg, b_fg, q_g2, k_g2, lb_logits)


def _fcum_kernel(ls_ref, nf_ref):
    blk = 256
    r = lax.broadcasted_iota(jnp.int32, (blk, blk), 0)
    c = lax.broadcasted_iota(jnp.int32, (blk, blk), 1)
    upper = (r <= c).astype(F32)
    carry = jnp.zeros((N_ATTN_HEADS, 1), F32)
    for j in range(SEQ // blk):
        x = ls_ref[0, :, j * blk:(j + 1) * blk]
        cs = jnp.dot(x, upper, precision=lax.Precision.HIGHEST,
                     preferred_element_type=F32) + carry
        nf_ref[0, :, j * blk:(j + 1) * blk] = cs * (-LOG2_E)
        carry = cs[:, blk - 1:blk]


def _fcum(ls):
    spec = pl.BlockSpec((1, N_ATTN_HEADS, SEQ), lambda b: (b, 0, 0))
    return pl.pallas_call(
        _fcum_kernel,
        grid=(BATCH,),
        in_specs=[spec],
        out_specs=spec,
        out_shape=jax.ShapeDtypeStruct((BATCH, N_ATTN_HEADS, SEQ), F32),
        compiler_params=pltpu.CompilerParams(dimension_semantics=("arbitrary",)),
        name="fcum",
    )(ls)


def _attn_kernel(q_ref, k_ref, v_ref, nf_ref, o_ref):
    hp = pl.program_id(1)
    lo = lax.broadcasted_iota(jnp.int32, (1, LANES), 1) < ATTN_HEAD_DIM
    r_i = lax.broadcasted_iota(jnp.int32, (TQ, TQ), 0)
    c_i = lax.broadcasted_iota(jnp.int32, (TQ, TQ), 1)
    causal = c_i <= r_i
    nt = (((1,), (1,)), ((), ()))

    for qi in range(SEQ // TQ):
        q0 = qi * TQ
        q = q_ref[0, q0:q0 + TQ, :]
        k_d = k_ref[0, q0:q0 + TQ, :]
        v_d = v_ref[0, q0:q0 + TQ, :]
        outs = []
        for a in range(2):
            qa = jnp.where(lo if a == 0 else jnp.logical_not(lo), q, jnp.zeros_like(q))
            nf = nf_ref[0, pl.ds(2 * hp + a, 1), :]
            s_d = lax.dot_general(qa, k_d, nt, preferred_element_type=F32) + nf[:, q0:q0 + TQ]
            s_d = jnp.where(causal, s_d, -jnp.inf)
            m = jnp.max(s_d, axis=-1, keepdims=True)
            if qi > 0:
                s_f = lax.dot_general(qa, k_ref[0, :q0, :], nt,
                                      preferred_element_type=F32) + nf[:, :q0]
                m = jnp.maximum(m, jnp.max(s_f, axis=-1, keepdims=True))
                p_f = jnp.exp2(s_f - m)
                l = jnp.sum(p_f, axis=-1, keepdims=True)
                acc = jnp.dot(p_f.astype(BF16), v_ref[0, :q0, :], preferred_element_type=F32)
            p_d = jnp.exp2(s_d - m)
            l_d = jnp.sum(p_d, axis=-1, keepdims=True)
            acc_d = jnp.dot(p_d.astype(BF16), v_d, preferred_element_type=F32)
            if qi > 0:
                l = l + l_d
                acc = acc + acc_d
            else:
                l, acc = l_d, acc_d
            outs.append(acc * (1.0 / l))
        o_ref[0, q0:q0 + TQ, :] = jnp.where(lo, outs[0], outs[1]).astype(o_ref.dtype)


def _attn(qa, ka, va, nf):
    qkv_spec = pl.BlockSpec((1, SEQ, LANES), lambda b, h: (b, 0, h))
    return pl.pallas_call(
        _attn_kernel,
        grid=(BATCH, D_ATTN // LANES),
        in_specs=[
            qkv_spec, qkv_spec, qkv_spec,
            pl.BlockSpec((1, N_ATTN_HEADS, SEQ), lambda b, h: (b, 0, 0)),
        ],
        out_specs=qkv_spec,
        out_shape=jax.ShapeDtypeStruct((BATCH, SEQ, D_ATTN), BF16),
        compiler_params=pltpu.CompilerParams(
            dimension_semantics=("arbitrary", "arbitrary"), vmem_limit_bytes=VMEM_LIMIT),
        name="attn",
    )(qa, ka, va, nf)


def _hgrn_kernel(qf_ref, kk_ref, iv_ref, lf_ref, gt_ref, g_ref, lbl_ref, o_ref, st_ref):
    c_len = HGRN_C
    st_ref[...] = jnp.zeros(st_ref.shape, F32)
    nt = (((1,), (1,)), ((), ()))
    tn = (((0,), (0,)), ((), ()))

    def finish(o, gate, t0, rows, sl):
        ms = jnp.mean(o * o, axis=-1, keepdims=True)
        y = o * lax.rsqrt(ms + EPS) * g_ref[:, sl]
        o_ref[0, pl.ds(t0, rows), sl] = (y * gate.astype(F32)).astype(o_ref.dtype)

    fc = HGRN_FAST_C
    fr = lax.broadcasted_iota(jnp.int32, (fc, fc), 0)
    fcol = lax.broadcasted_iota(jnp.int32, (fc, fc), 1)
    f_tril = fr >= fcol
    f_tril_bf = f_tril.astype(BF16)

    heads = range(N_HGRN_HEADS)
    sls = [slice(hd * HGRN_K, (hd + 1) * HGRN_K) for hd in heads]

    def fast_prepare(t0):
        lf = lf_ref[0, pl.ds(t0, fc), :]
        qf = qf_ref[0, pl.ds(t0, fc), :].astype(F32)
        kk = kk_ref[0, pl.ds(t0, fc), :].astype(F32)
        iv = iv_ref[0, pl.ds(t0, fc), :]
        hi = lf.astype(BF16)
        r1 = lf - hi.astype(F32)
        mid = r1.astype(BF16)
        lo = (r1 - mid.astype(F32)).astype(BF16)
        b = (jnp.dot(f_tril_bf, hi, preferred_element_type=F32)
             + jnp.dot(f_tril_bf, mid, preferred_element_type=F32)
             + jnp.dot(f_tril_bf, lo, preferred_element_type=F32))
        b_mid = b[fc // 2 - 1:fc // 2, :]
        b_last = b[fc - 1:fc, :]
        d = b - b_mid
        qt = qf * jnp.exp(d)
        kt = kk * jnp.exp(-d)
        qi = (qt * jnp.exp(b_mid)).astype(BF16)
        kd = (kt * jnp.exp(b_last - b_mid)).astype(BF16)
        qt = qt.astype(BF16)
        kt = kt.astype(BF16)
        o_intra, st_inc = [], []
        for sl in sls:
            a = lax.dot_general(qt[:, sl], kt[:, sl], nt, preferred_element_type=F32)
            a = jnp.where(f_tril, a, 0.0).astype(BF16)
            o_intra.append(jnp.dot(a, iv[:, sl], preferred_element_type=F32))
            st_inc.append(lax.dot_general(iv[:, sl], kd[:, sl], tn, preferred_element_type=F32))
        return qi, o_intra, st_inc, jnp.exp(b_last), gt_ref[0, pl.ds(t0, fc), :]

    def fast_step(ci, _):
        base = ci * (HGRN_FAST_UNROLL * fc)
        t0s = [pl.multiple_of(base + u * fc, fc) for u in range(HGRN_FAST_UNROLL)]
        prepared = [fast_prepare(t0) for t0 in t0s]
        st = [st_ref[hd] for hd in heads]
        outs = []
        for qi, o_intra, st_inc, st_decay, gate in prepared:
            outs.append([o_intra[hd] + lax.dot_general(
                qi[:, sl], st[hd].astype(BF16), nt, preferred_element_type=F32)
                for hd, sl in zip(heads, sls)])
            st = [st[hd] * st_decay[:, sl] + st_inc[hd] for hd, sl in zip(heads, sls)]
        for hd in heads:
            st_ref[hd] = st[hd]
        for t0, o, (_, _, _, _, gate) in zip(t0s, outs, prepared):
            for hd, sl in zip(heads, sls):
                finish(o[hd], gate[:, sl], t0, fc, sl)
        return 0


    row = lax.broadcasted_iota(jnp.int32, (c_len, LANES), 0)
    r_i = lax.broadcasted_iota(jnp.int32, (c_len, c_len), 0)
    c_i = lax.broadcasted_iota(jnp.int32, (c_len, c_len), 1)
    halves = []
    hh = SUBLANES
    while hh < c_len:
        halves.append(hh)
        hh *= 2
    lvl_masks = [
        ((r_i // (2 * h)) == (c_i // (2 * h))) & ((r_i % (2 * h)) >= h) & ((c_i % (2 * h)) < h)
        for h in halves
    ]
    diag_masks = [
        (c_i == (r_i // SUBLANES) * SUBLANES + j) & ((r_i % SUBLANES) >= j)
        for j in range(SUBLANES)
    ]
    def rows_bcast(x, group, idx):
        if group == c_len:
            return jnp.broadcast_to(x[idx:idx + 1, :], x.shape)
        x3 = x.reshape(c_len // group, group, LANES)
        return jnp.broadcast_to(x3[:, idx:idx + 1, :], x3.shape).reshape(c_len, LANES)

    def robust_chunk(ci, _):
        t0 = pl.multiple_of(ci * c_len, c_len)
        for hd in range(N_HGRN_HEADS):
            sl = slice(hd * HGRN_K, (hd + 1) * HGRN_K)
            lf = lf_ref[0, pl.ds(t0, c_len), sl]
            qf = qf_ref[0, pl.ds(t0, c_len), sl].astype(F32)
            kk = kk_ref[0, pl.ds(t0, c_len), sl].astype(F32)
            iv = iv_ref[0, pl.ds(t0, c_len), sl]
            b = lf
            k = 1
            while k < c_len:
                b = b + jnp.where(row >= k, pltpu.roll(b, k, 0), 0.0)
                k *= 2
            b_last = b[c_len - 1:c_len, :]
            st = st_ref[hd]
            qi = (qf * jnp.exp(b)).astype(BF16)
            o = lax.dot_general(qi, st.astype(BF16), nt, preferred_element_type=F32)
            a = jnp.zeros((c_len, c_len), F32)
            for h, msk in zip(halves, lvl_masks):
                ref = rows_bcast(b, 2 * h, h - 1)
                ql = (qf * jnp.exp(jnp.minimum(b - ref, 0.0))).astype(BF16)
                kl = (kk * jnp.exp(jnp.minimum(ref - b, 0.0))).astype(BF16)
                al = lax.dot_general(ql, kl, nt, preferred_element_type=F32)
                a = jnp.where(msk, al, a)
            for j in range(SUBLANES):
                bs = rows_bcast(b, SUBLANES, j)
                ks = rows_bcast(kk, SUBLANES, j)
                e = jnp.exp(jnp.minimum(b - bs, 0.0))
                col = jnp.sum(qf * ks * e, axis=-1, keepdims=True)
                a = jnp.where(diag_masks[j], col, a)
            o = o + jnp.dot(a.astype(BF16), iv, preferred_element_type=F32)
            kd = (kk * jnp.exp(b_last - b)).astype(BF16)
            st_ref[hd] = st * jnp.exp(b_last) + lax.dot_general(
                iv, kd, tn, preferred_element_type=F32)
            finish(o, gt_ref[0, pl.ds(t0, c_len), sl], t0, c_len, sl)
        return 0

    lbl = lbl_ref[...]
    e = jnp.exp(lbl - jnp.max(lbl, axis=0, keepdims=True))
    lb = e[0:1, :] / jnp.sum(e, axis=0, keepdims=True)
    worst_exponent = (HGRN_FAST_C // 2) * jnp.max(-jnp.log(lb))
    fast = worst_exponent <= HGRN_FAST_MAX_EXPONENT

    @pl.when(fast)
    def _():
        lax.fori_loop(0, SEQ // (HGRN_FAST_UNROLL * fc), fast_step, 0)

    @pl.when(jnp.logical_not(fast))
    def _():
        lax.fori_loop(0, SEQ // c_len, robust_chunk, 0)


def _hgrn(qf, kk, iv, lf, gt, g, lb_logits):
    spec = pl.BlockSpec((1, SEQ, D_HGRN), lambda b: (b, 0, 0))
    return pl.pallas_call(
        _hgrn_kernel,
        grid=(BATCH,),
        in_specs=[spec, spec, spec, spec, spec,
                  pl.BlockSpec((1, D_HGRN), lambda b: (0, 0)),
                  pl.BlockSpec((2, D_HGRN), lambda b: (0, 0))],
        out_specs=spec,
        out_shape=jax.ShapeDtypeStruct((BATCH, SEQ, D_HGRN), BF16),
        scratch_shapes=[pltpu.VMEM((N_HGRN_HEADS, HGRN_K, HGRN_K), F32)],
        compiler_params=pltpu.CompilerParams(
            dimension_semantics=("arbitrary",), vmem_limit_bytes=VMEM_LIMIT),
        name="hgrn",
    )(qf, kk, iv, lf, gt, g, lb_logits)


def kernel(x, c, w_ada, b_ada, g_norm1, ffn1_w_in, ffn1_w_out, g_norm_mix, w_in_mix, b_fgate,
           q_norm_g, k_norm_g, attn_out_g, hgrn_lb_logits, hgrn_out_g, w_out_mix, g_norm2,
           ffn2_w_in, ffn2_w_out):
    assert x.shape == (BATCH, SEQ, D_MODEL) and w_ada.shape[0] == 1
    t = BATCH * SEQ
    mod = _adaln(c, w_ada[0], b_ada).reshape(BATCH, N_MOD, D_MODEL)

    x0 = x.reshape(t, D_MODEL)
    x1 = _ffn(x0, mod, g_norm1, ffn1_w_in[0].astype(BF16), ffn1_w_out[0].astype(BF16), 0, "ffn1")

    w_mix = w_in_mix[0]
    fg0 = 3 * D_ATTN
    w_main = jnp.concatenate([w_mix[:, :fg0], w_mix[:, fg0 + N_ATTN_HEADS:]], axis=1).astype(BF16)
    w_fg = jnp.pad(w_mix[:, fg0:fg0 + N_ATTN_HEADS], ((0, 0), (0, LANES - N_ATTN_HEADS))).astype(BF16)
    b_fg = jnp.broadcast_to(b_fgate[0][:, None], (N_ATTN_HEADS, LANES))
    q_g2 = jnp.tile(q_norm_g, (1, LANES // ATTN_HEAD_DIM))
    k_g2 = jnp.tile(k_norm_g, (1, LANES // ATTN_HEAD_DIM))
    qa, ka, va, ls, qf, lf, kk, iv, gt = _mix_in(
        x1, mod, g_norm_mix, w_main, w_fg, b_fg, q_g2, k_g2, hgrn_lb_logits)

    nf = _fcum(ls)
    seq3 = lambda a: a.reshape(BATCH, SEQ, a.shape[-1])
    oa = _attn(seq3(qa), seq3(ka), seq3(va), nf)
    oh = _hgrn(seq3(qf), seq3(kk), seq3(iv), seq3(lf), seq3(gt), hgrn_out_g,
               hgrn_lb_logits)

    mixer_out = (oa.reshape(t, D_ATTN), oh.reshape(t, D_HGRN), attn_out_g, w_out_mix[0].astype(BF16))
    x3 = _ffn(x1, mod, g_norm2, ffn2_w_in[0].astype(BF16), ffn2_w_out[0].astype(BF16), 6, "ffn2",
              mixer_out=mixer_out)
    return x3.reshape(BATCH, SEQ, D_MODEL)
```

```python
import functools

import jax
import jax.numpy as jnp
from jax import lax
from jax.experimental import pallas as pl
from jax.experimental.pallas import tpu as pltpu

D_MODEL = 1024
BATCH = 32
SEQ = 2048
D_ATTN = 512
D_HGRN = 512
ATTN_HEAD_DIM = 64
N_ATTN_HEADS = 8
HGRN_K = 128
N_HGRN_HEADS = 4
D_FF = 2816
N_MOD = 9
FFN_RES = 0.5
EPS = 1e-6
LOG2_E = 1.4426950408889634

LANES = 128
SUBLANES = 8
N_MAIN = 3 * D_ATTN + 4 * D_HGRN

F32 = jnp.float32
BF16 = jnp.bfloat16

TM_FFN = 512
TM_MIX = 512
TQ = 256
ATTN_MAX_SCORE_BOUND = 100.0
HGRN_C = 64
HGRN_FAST_C = 128
HGRN_FAST_UNROLL = 4
HGRN_FAST_MAX_EXPONENT = 70.0
VMEM_LIMIT = 56 * 1024 * 1024


def _sigmoid(x):
    return 1.0 / (1.0 + jnp.exp(-x))


def _silu(x):
    return x * _sigmoid(x)


def _modulated_rms(x, g, shift, scale):
    ms = jnp.mean(x * x, axis=-1, keepdims=True)
    y = x * lax.rsqrt(ms + EPS) * g
    return y * (1.0 + scale) + shift


def _adaln_kernel(c_ref, w_ref, b_ref, o_ref):
    cs = _silu(c_ref[...]).astype(BF16)
    w = w_ref[...].astype(BF16)
    o_ref[...] = jnp.dot(cs, w, preferred_element_type=F32) + b_ref[...]


def _adaln(c, w, b):
    n = w.shape[1]
    tn = D_MODEL
    return pl.pallas_call(
        _adaln_kernel,
        grid=(n // tn,),
        in_specs=[
            pl.BlockSpec((BATCH, D_MODEL), lambda j: (0, 0)),
            pl.BlockSpec((D_MODEL, tn), lambda j: (0, j)),
            pl.BlockSpec((1, tn), lambda j: (0, j)),
        ],
        out_specs=pl.BlockSpec((BATCH, tn), lambda j: (0, j)),
        out_shape=jax.ShapeDtypeStruct((BATCH, n), F32),
        compiler_params=pltpu.CompilerParams(dimension_semantics=("arbitrary",)),
        name="adaln",
    )(c, w, b)


def _ffn_kernel(*refs, mod_base, with_mixer_out):
    if with_mixer_out:
        x_ref, mod_ref, g_ref, win_ref, wout_ref, oa_ref, oh_ref, ag_ref, wmo_ref, o_ref = refs
        oa = oa_ref[...].astype(F32)
        ms = jnp.mean(oa * oa, axis=-1, keepdims=True)
        oa = (oa * lax.rsqrt(ms + EPS) * ag_ref[...]).astype(BF16)
        cat = jnp.concatenate([oa, oh_ref[...]], axis=-1)
        mixed = jnp.dot(cat, wmo_ref[...], preferred_element_type=F32)
        x = x_ref[...] + mod_ref[0, 5:6, :] * mixed
    else:
        x_ref, mod_ref, g_ref, win_ref, wout_ref, o_ref = refs
        x = x_ref[...]
    shift = mod_ref[0, mod_base:mod_base + 1, :]
    scale = mod_ref[0, mod_base + 1:mod_base + 2, :]
    gate = mod_ref[0, mod_base + 2:mod_base + 3, :]
    h = _modulated_rms(x, g_ref[...], shift, scale).astype(BF16)
    gu = jnp.dot(h, win_ref[...], preferred_element_type=F32)
    a = (_silu(gu[:, :D_FF]) * gu[:, D_FF:]).astype(BF16)
    out = jnp.dot(a, wout_ref[...], preferred_element_type=F32)
    o_ref[...] = x + (FFN_RES * gate) * out


def _resident(shape):
    return pl.BlockSpec(shape, lambda i: (0,) * len(shape), pipeline_mode=pl.Buffered(1))


def _ffn(x2d, mod, g, w_in, w_out, mod_base, name, mixer_out=None):
    t = x2d.shape[0]
    tiles_per_seq = SEQ // TM_FFN
    row_blk = lambda n: pl.BlockSpec((TM_FFN, n), lambda i: (i, 0))
    in_specs = [
        row_blk(D_MODEL),
        pl.BlockSpec((1, N_MOD, D_MODEL), lambda i: (i // tiles_per_seq, 0, 0)),
        _resident((1, D_MODEL)),
        _resident((D_MODEL, 2 * D_FF)),
        _resident((D_FF, D_MODEL)),
    ]
    args = [x2d, mod, g, w_in, w_out]
    if mixer_out is not None:
        in_specs += [row_blk(D_ATTN), row_blk(D_HGRN), _resident((1, D_ATTN)),
                     _resident((D_MODEL, D_MODEL))]
        args += list(mixer_out)
    return pl.pallas_call(
        functools.partial(_ffn_kernel, mod_base=mod_base, with_mixer_out=mixer_out is not None),
        grid=(t // TM_FFN,),
        in_specs=in_specs,
        out_specs=row_blk(D_MODEL),
        out_shape=jax.ShapeDtypeStruct((t, D_MODEL), F32),
        compiler_params=pltpu.CompilerParams(
            dimension_semantics=("arbitrary",), vmem_limit_bytes=VMEM_LIMIT),
        name=name,
    )(*args)


def _mix_in_kernel(x_ref, mod_ref, g_ref, w_ref, wfg_ref, bfg_ref, qg_ref, kg_ref, lbl_ref,
                   qa_ref, ka_ref, va_ref, ls_ref, qf_ref, lf_ref, kk_ref, iv_ref, gt_ref):
    x = x_ref[...]
    h = _modulated_rms(x, g_ref[...], mod_ref[0, 3:4, :], mod_ref[0, 4:5, :]).astype(BF16)
    p = jnp.dot(h, w_ref[...], preferred_element_type=F32)

    lo = lax.broadcasted_iota(jnp.int32, (1, LANES), 1) < ATTN_HEAD_DIM
    inv_dh = 1.0 / ATTN_HEAD_DIM

    def head_norm(t, gain):
        sq = t * t
        s_lo = jnp.sum(jnp.where(lo, sq, 0.0), axis=-1, keepdims=True)
        s_hi = jnp.sum(jnp.where(lo, 0.0, sq), axis=-1, keepdims=True)
        r = jnp.where(lo, lax.rsqrt(s_lo * inv_dh + EPS), lax.rsqrt(s_hi * inv_dh + EPS))
        return (t * r * gain).astype(BF16)

    q_gain = qg_ref[...] * (ATTN_HEAD_DIM ** -0.5 * LOG2_E)
    k_gain = kg_ref[...]
    for j in range(D_ATTN // LANES):
        sl = slice(j * LANES, (j + 1) * LANES)
        qa_ref[:, sl] = head_norm(p[:, j * LANES:(j + 1) * LANES], q_gain)
        ka_ref[:, sl] = head_norm(p[:, D_ATTN + j * LANES:D_ATTN + (j + 1) * LANES], k_gain)
    va_ref[...] = p[:, 2 * D_ATTN:3 * D_ATTN].astype(BF16)

    fg = jnp.dot(h, wfg_ref[...], preferred_element_type=F32)
    fgt = fg.T[:N_ATTN_HEADS, :] + bfg_ref[:, 0:1]
    ls_ref[...] = jnp.minimum(fgt, 0.0) - jnp.log1p(jnp.exp(-jnp.abs(fgt)))

    lbl = lbl_ref[...]
    e = jnp.exp(lbl - jnp.max(lbl, axis=0, keepdims=True))
    lb = e[0:1, :] / jnp.sum(e, axis=0, keepdims=True)
    o0 = 3 * D_ATTN
    qh = p[:, o0:o0 + D_HGRN]
    z = p[:, o0 + D_HGRN:o0 + 2 * D_HGRN]
    ih = p[:, o0 + 2 * D_HGRN:o0 + 3 * D_HGRN]
    gh = p[:, o0 + 3 * D_HGRN:o0 + 4 * D_HGRN]
    t = jnp.exp(-jnp.abs(z))
    r = 1.0 / (1.0 + t)
    tr = t * r
    pos = z >= 0.0
    sig_p = jnp.where(pos, r, tr)
    sig_n = jnp.where(pos, tr, r)
    lf_ref[...] = jnp.log(lb + (1.0 - lb) * sig_p)
    kk_ref[...] = ((1.0 - lb) * sig_n).astype(BF16)
    qf_ref[...] = _silu(qh).astype(BF16)
    iv_ref[...] = ih.astype(BF16)
    gt_ref[...] = _silu(gh).astype(BF16)


def _mix_in(x2d, mod, g, w_main, w_fg, b_fg, q_g2, k_g2, lb_logits):
    t = x2d.shape[0]
    tm = TM_MIX
    tiles_per_seq = SEQ // tm
    row_blk = lambda n: pl.BlockSpec((tm, n), lambda i: (i, 0))
    bf = lambda n: jax.ShapeDtypeStruct((t, n), BF16)
    return pl.pallas_call(
        _mix_in_kernel,
        grid=(t // tm,),
        in_specs=[
            row_blk(D_MODEL),
            pl.BlockSpec((1, N_MOD, D_MODEL), lambda i: (i // tiles_per_seq, 0, 0)),
            _resident((1, D_MODEL)),
            _resident((D_MODEL, N_MAIN)),
            _resident((D_MODEL, LANES)),
            _resident((N_ATTN_HEADS, LANES)),
            _resident((1, LANES)),
            _resident((1, LANES)),
            _resident((2, D_HGRN)),
        ],
        out_specs=[
            row_blk(D_ATTN), row_blk(D_ATTN), row_blk(D_ATTN),
            pl.BlockSpec((None, N_ATTN_HEADS, tm),
                         lambda i: (i // tiles_per_seq, 0, i % tiles_per_seq)),
            row_blk(D_HGRN), row_blk(D_HGRN), row_blk(D_HGRN), row_blk(D_HGRN), row_blk(D_HGRN),
        ],
        out_shape=[
            bf(D_ATTN), bf(D_ATTN), bf(D_ATTN),
            jax.ShapeDtypeStruct((BATCH, N_ATTN_HEADS, SEQ), F32),
            bf(D_HGRN), jax.ShapeDtypeStruct((t, D_HGRN), F32), bf(D_HGRN), bf(D_HGRN), bf(D_HGRN),
        ],
        compiler_params=pltpu.CompilerParams(
            dimension_semantics=("arbitrary",), vmem_limit_bytes=VMEM_LIMIT),
        name="mix_in",
    )(x2d, mod, g, w_main, w_fg, b_fg, q_g2, k_g2, lb_logits)


def _fcum_kernel(ls_ref, nf_ref):
    blk = 256
    r = lax.broadcasted_iota(jnp.int32, (blk, blk), 0)
    c = lax.broadcasted_iota(jnp.int32, (blk, blk), 1)
    upper = (r <= c).astype(F32)
    carry = jnp.zeros((N_ATTN_HEADS, 1), F32)
    for j in range(SEQ // blk):
        x = ls_ref[0, :, j * blk:(j + 1) * blk]
        cs = jnp.dot(x, upper, precision=lax.Precision.HIGHEST,
                     preferred_element_type=F32) + carry
        nf_ref[0, :, j * blk:(j + 1) * blk] = cs * (-LOG2_E)
        carry = cs[:, blk - 1:blk]


def _fcum(ls):
    spec = pl.BlockSpec((1, N_ATTN_HEADS, SEQ), lambda b: (b, 0, 0))
    return pl.pallas_call(
        _fcum_kernel,
        grid=(BATCH,),
        in_specs=[spec],
        out_specs=spec,
        out_shape=jax.ShapeDtypeStruct((BATCH, N_ATTN_HEADS, SEQ), F32),
        compiler_params=pltpu.CompilerParams(dimension_semantics=("arbitrary",)),
        name="fcum",
    )(ls)


def _attn_kernel(q_ref, k_ref, v_ref, nf_ref, qg_ref, kg_ref, o_ref):
    hp = pl.program_id(1)
    lane = lax.broadcasted_iota(jnp.int32, (1, LANES), 1)
    lo = lane < ATTN_HEAD_DIM
    r_i = lax.broadcasted_iota(jnp.int32, (TQ, TQ), 0)
    c_i = lax.broadcasted_iota(jnp.int32, (TQ, TQ), 1)
    causal = c_i <= r_i
    nt = (((1,), (1,)), ((), ()))
    v1 = jnp.concatenate([v_ref[0], jnp.ones((SEQ, LANES), BF16)], axis=1)

    def attend(bounded):
        if bounded:
            nf8 = nf_ref[0]
            nf_t = jnp.concatenate(
                [nf8, jnp.zeros((LANES - N_ATTN_HEADS, SEQ), F32)], axis=0).T
        for qi in range(SEQ // TQ):
            q0 = qi * TQ
            q = q_ref[0, q0:q0 + TQ, :]
            k_d = k_ref[0, q0:q0 + TQ, :]
            v_d = v1[q0:q0 + TQ, :]
            outs = []
            for a in range(2):
                head = 2 * hp + a
                qa = jnp.where(lo if a == 0 else jnp.logical_not(lo), q, jnp.zeros_like(q))
                nf = nf_ref[0, pl.ds(head, 1), :]
                s_d = lax.dot_general(qa, k_d, nt, preferred_element_type=F32) + nf[:, q0:q0 + TQ]
                s_d = jnp.where(causal, s_d, -jnp.inf)
                if qi > 0:
                    s_f = lax.dot_general(qa, k_ref[0, :q0, :], nt,
                                          preferred_element_type=F32) + nf[:, :q0]
                if bounded:
                    m = jnp.sum(jnp.where(lane == head, nf_t[q0:q0 + TQ, :], 0.0),
                                axis=-1, keepdims=True)
                else:
                    m = jnp.max(s_d, axis=-1, keepdims=True)
                    if qi > 0:
                        m = jnp.maximum(m, jnp.max(s_f, axis=-1, keepdims=True))
                acc = jnp.dot(jnp.exp2(s_d - m).astype(BF16), v_d, preferred_element_type=F32)
                if qi > 0:
                    acc = acc + jnp.dot(jnp.exp2(s_f - m).astype(BF16), v1[:q0, :],
                                        preferred_element_type=F32)
                outs.append(acc[:, :LANES] * (1.0 / acc[:, LANES:]))
            o_ref[0, q0:q0 + TQ, :] = jnp.where(lo, outs[0], outs[1]).astype(o_ref.dtype)

    bound = (ATTN_HEAD_DIM * ATTN_HEAD_DIM ** -0.5 * LOG2_E * 1.02) * (
        jnp.max(jnp.abs(qg_ref[...])) * jnp.max(jnp.abs(kg_ref[...])))
    bounded = bound <= ATTN_MAX_SCORE_BOUND

    @pl.when(bounded)
    def _():
        attend(True)

    @pl.when(jnp.logical_not(bounded))
    def _():
        attend(False)


def _attn(qa, ka, va, nf, q_g2, k_g2):
    qkv_spec = pl.BlockSpec((1, SEQ, LANES), lambda b, h: (b, 0, h))
    gain_spec = pl.BlockSpec((1, LANES), lambda b, h: (0, 0))
    return pl.pallas_call(
        _attn_kernel,
        grid=(BATCH, D_ATTN // LANES),
        in_specs=[
            qkv_spec, qkv_spec, qkv_spec,
            pl.BlockSpec((1, N_ATTN_HEADS, SEQ), lambda b, h: (b, 0, 0)),
            gain_spec, gain_spec,
        ],
        out_specs=qkv_spec,
        out_shape=jax.ShapeDtypeStruct((BATCH, SEQ, D_ATTN), BF16),
        compiler_params=pltpu.CompilerParams(
            dimension_semantics=("arbitrary", "arbitrary"), vmem_limit_bytes=VMEM_LIMIT),
        name="attn",
    )(qa, ka, va, nf, q_g2, k_g2)


def _hgrn_kernel(qf_ref, kk_ref, iv_ref, lf_ref, gt_ref, g_ref, lbl_ref, o_ref, st_ref):
    c_len = HGRN_C
    st_ref[...] = jnp.zeros(st_ref.shape, F32)
    nt = (((1,), (1,)), ((), ()))
    tn = (((0,), (0,)), ((), ()))

    def finish(o, gate, t0, rows, sl):
        ms = jnp.mean(o * o, axis=-1, keepdims=True)
        y = o * lax.rsqrt(ms + EPS) * g_ref[:, sl]
        o_ref[0, pl.ds(t0, rows), sl] = (y * gate.astype(F32)).astype(o_ref.dtype)

    fc = HGRN_FAST_C
    fr = lax.broadcasted_iota(jnp.int32, (fc, fc), 0)
    fcol = lax.broadcasted_iota(jnp.int32, (fc, fc), 1)
    f_tril = fr >= fcol
    f_tril_bf = f_tril.astype(BF16)

    heads = range(N_HGRN_HEADS)
    sls = [slice(hd * HGRN_K, (hd + 1) * HGRN_K) for hd in heads]

    def fast_prepare(t0):
        lf = lf_ref[0, pl.ds(t0, fc), :]
        qf = qf_ref[0, pl.ds(t0, fc), :].astype(F32)
        kk = kk_ref[0, pl.ds(t0, fc), :].astype(F32)
        iv = iv_ref[0, pl.ds(t0, fc), :]
        hi = lf.astype(BF16)
        r1 = lf - hi.astype(F32)
        mid = r1.astype(BF16)
        lo = (r1 - mid.astype(F32)).astype(BF16)
        b = (jnp.dot(f_tril_bf, hi, preferred_element_type=F32)
             + jnp.dot(f_tril_bf, mid, preferred_element_type=F32)
             + jnp.dot(f_tril_bf, lo, preferred_element_type=F32))
        b_mid = b[fc // 2 - 1:fc // 2, :]
        b_last = b[fc - 1:fc, :]
        d = b - b_mid
        qt = qf * jnp.exp(d)
        kt = kk * jnp.exp(-d)
        qi = (qt * jnp.exp(b_mid)).astype(BF16)
        kd = (kt * jnp.exp(b_last - b_mid)).astype(BF16)
        qt = qt.astype(BF16)
        kt = kt.astype(BF16)
        o_intra, st_inc = [], []
        for sl in sls:
            a = lax.dot_general(qt[:, sl], kt[:, sl], nt, preferred_element_type=F32)
            a = jnp.where(f_tril, a, 0.0).astype(BF16)
            o_intra.append(jnp.dot(a, iv[:, sl], preferred_element_type=F32))
            st_inc.append(lax.dot_general(iv[:, sl], kd[:, sl], tn, preferred_element_type=F32))
        return qi, o_intra, st_inc, jnp.exp(b_last), gt_ref[0, pl.ds(t0, fc), :]

    def fast_step(ci, _):
        base = ci * (HGRN_FAST_UNROLL * fc)
        t0s = [pl.multiple_of(base + u * fc, fc) for u in range(HGRN_FAST_UNROLL)]
        prepared = [fast_prepare(t0) for t0 in t0s]
        st = [st_ref[hd] for hd in heads]
        outs = []
        for qi, o_intra, st_inc, st_decay, gate in prepared:
            outs.append([o_intra[hd] + lax.dot_general(
                qi[:, sl], st[hd].astype(BF16), nt, preferred_element_type=F32)
                for hd, sl in zip(heads, sls)])
            st = [st[hd] * st_decay[:, sl] + st_inc[hd] for hd, sl in zip(heads, sls)]
        for hd in heads:
            st_ref[hd] = st[hd]
        for t0, o, (_, _, _, _, gate) in zip(t0s, outs, prepared):
            for hd, sl in zip(heads, sls):
                finish(o[hd], gate[:, sl], t0, fc, sl)
        return 0


    row = lax.broadcasted_iota(jnp.int32, (c_len, LANES), 0)
    r_i = lax.broadcasted_iota(jnp.int32, (c_len, c_len), 0)
    c_i = lax.broadcasted_iota(jnp.int32, (c_len, c_len), 1)
    halves = []
    hh = SUBLANES
    while hh < c_len:
        halves.append(hh)
        hh *= 2
    lvl_masks = [
        ((r_i // (2 * h)) == (c_i // (2 * h))) & ((r_i % (2 * h)) >= h) & ((c_i % (2 * h)) < h)
        for h in halves
    ]
    diag_masks = [
        (c_i == (r_i // SUBLANES) * SUBLANES + j) & ((r_i % SUBLANES) >= j)
        for j in range(SUBLANES)
    ]
    def rows_bcast(x, group, idx):
        if group == c_len:
            return jnp.broadcast_to(x[idx:idx + 1, :], x.shape)
        x3 = x.reshape(c_len // group, group, LANES)
        return jnp.broadcast_to(x3[:, idx:idx + 1, :], x3.shape).reshape(c_len, LANES)

    def robust_chunk(ci, _):
        t0 = pl.multiple_of(ci * c_len, c_len)
        for hd in range(N_HGRN_HEADS):
            sl = slice(hd * HGRN_K, (hd + 1) * HGRN_K)
            lf = lf_ref[0, pl.ds(t0, c_len), sl]
            qf = qf_ref[0, pl.ds(t0, c_len), sl].astype(F32)
            kk = kk_ref[0, pl.ds(t0, c_len), sl].astype(F32)
            iv = iv_ref[0, pl.ds(t0, c_len), sl]
            b = lf
            k = 1
            while k < c_len:
                b = b + jnp.where(row >= k, pltpu.roll(b, k, 0), 0.0)
                k *= 2
            b_last = b[c_len - 1:c_len, :]
            st = st_ref[hd]
            qi = (qf * jnp.exp(b)).astype(BF16)
            o = lax.dot_general(qi, st.astype(BF16), nt, preferred_element_type=F32)
            a = jnp.zeros((c_len, c_len), F32)
            for h, msk in zip(halves, lvl_masks):
                ref = rows_bcast(b, 2 * h, h - 1)
                ql = (qf * jnp.exp(jnp.minimum(b - ref, 0.0))).astype(BF16)
                kl = (kk * jnp.exp(jnp.minimum(ref - b, 0.0))).astype(BF16)
                al = lax.dot_general(ql, kl, nt, preferred_element_type=F32)
                a = jnp.where(msk, al, a)
            for j in range(SUBLANES):
                bs = rows_bcast(b, SUBLANES, j)
                ks = rows_bcast(kk, SUBLANES, j)
                e = jnp.exp(jnp.minimum(b - bs, 0.0))
                col = jnp.sum(qf * ks * e, axis=-1, keepdims=True)
                a = jnp.where(diag_masks[j], col, a)
            o = o + jnp.dot(a.astype(BF16), iv, preferred_element_type=F32)
            kd = (kk * jnp.exp(b_last - b)).astype(BF16)
            st_ref[hd] = st * jnp.exp(b_last) + lax.dot_general(
                iv, kd, tn, preferred_element_type=F32)
            finish(o, gt_ref[0, pl.ds(t0, c_len), sl], t0, c_len, sl)
        return 0

    lbl = lbl_ref[...]
    e = jnp.exp(lbl - jnp.max(lbl, axis=0, keepdims=True))
    lb = e[0:1, :] / jnp.sum(e, axis=0, keepdims=True)
    worst_exponent = (HGRN_FAST_C // 2) * jnp.max(-jnp.log(lb))
    fast = worst_exponent <= HGRN_FAST_MAX_EXPONENT

    @pl.when(fast)
    def _():
        lax.fori_loop(0, SEQ // (HGRN_FAST_UNROLL * fc), fast_step, 0)

    @pl.when(jnp.logical_not(fast))
    def _():
        lax.fori_loop(0, SEQ // c_len, robust_chunk, 0)


def _hgrn(qf, kk, iv, lf, gt, g, lb_logits):
    spec = pl.BlockSpec((1, SEQ, D_HGRN), lambda b: (b, 0, 0))
    return pl.pallas_call(
        _hgrn_kernel,
        grid=(BATCH,),
        in_specs=[spec, spec, spec, spec, spec,
                  pl.BlockSpec((1, D_HGRN), lambda b: (0, 0)),
                  pl.BlockSpec((2, D_HGRN), lambda b: (0, 0))],
        out_specs=spec,
        out_shape=jax.ShapeDtypeStruct((BATCH, SEQ, D_HGRN), BF16),
        scratch_shapes=[pltpu.VMEM((N_HGRN_HEADS, HGRN_K, HGRN_K), F32)],
        compiler_params=pltpu.CompilerParams(
            dimension_semantics=("arbitrary",), vmem_limit_bytes=VMEM_LIMIT),
        name="hgrn",
    )(qf, kk, iv, lf, gt, g, lb_logits)


def kernel(x, c, w_ada, b_ada, g_norm1, ffn1_w_in, ffn1_w_out, g_norm_mix, w_in_mix, b_fgate,
           q_norm_g, k_norm_g, attn_out_g, hgrn_lb_logits, hgrn_out_g, w_out_mix, g_norm2,
           ffn2_w_in, ffn2_w_out):
    assert x.shape == (BATCH, SEQ, D_MODEL) and w_ada.shape[0] == 1
    t = BATCH * SEQ
    mod = _adaln(c, w_ada[0], b_ada).reshape(BATCH, N_MOD, D_MODEL)

    x0 = x.reshape(t, D_MODEL)
    x1 = _ffn(x0, mod, g_norm1, ffn1_w_in[0].astype(BF16), ffn1_w_out[0].astype(BF16), 0, "ffn1")

    w_mix = w_in_mix[0]
    fg0 = 3 * D_ATTN
    w_main = jnp.concatenate([w_mix[:, :fg0], w_mix[:, fg0 + N_ATTN_HEADS:]], axis=1).astype(BF16)
    w_fg = jnp.pad(w_mix[:, fg0:fg0 + N_ATTN_HEADS], ((0, 0), (0, LANES - N_ATTN_HEADS))).astype(BF16)
    b_fg = jnp.broadcast_to(b_fgate[0][:, None], (N_ATTN_HEADS, LANES))
    q_g2 = jnp.tile(q_norm_g, (1, LANES // ATTN_HEAD_DIM))
    k_g2 = jnp.tile(k_norm_g, (1, LANES // ATTN_HEAD_DIM))
    qa, ka, va, ls, qf, lf, kk, iv, gt = _mix_in(
        x1, mod, g_norm_mix, w_main, w_fg, b_fg, q_g2, k_g2, hgrn_lb_logits)

    nf = _fcum(ls)
    seq3 = lambda a: a.reshape(BATCH, SEQ, a.shape[-1])
    oa = _attn(seq3(qa), seq3(ka), seq3(va), nf, q_g2, k_g2)
    oh = _hgrn(seq3(qf), seq3(kk), seq3(iv), seq3(lf), seq3(gt), hgrn_out_g,
               hgrn_lb_logits)

    mixer_out = (oa.reshape(t, D_ATTN), oh.reshape(t, D_HGRN), attn_out_g, w_out_mix[0].astype(BF16))
    x3 = _ffn(x1, mod, g_norm2, ffn2_w_in[0].astype(BF16), ffn2_w_out[0].astype(BF16), 6, "ffn2",
              mixer_out=mixer_out)
    return x3.reshape(BATCH, SEQ, D_MODEL)
```

```python
import functools

import jax
import jax.numpy as jnp
from jax import lax
from jax.experimental import pallas as pl
from jax.experimental.pallas import tpu as pltpu

D_MODEL = 1024
BATCH = 32
SEQ = 2048
D_ATTN = 512
D_HGRN = 512
ATTN_HEAD_DIM = 64
N_ATTN_HEADS = 8
HGRN_K = 128
N_HGRN_HEADS = 4
D_FF = 2816
N_MOD = 9
FFN_RES = 0.5
EPS = 1e-6
LOG2_E = 1.4426950408889634

LANES = 128
SUBLANES = 8
N_MAIN = 3 * D_ATTN + 4 * D_HGRN

F32 = jnp.float32
BF16 = jnp.bfloat16

TM_FFN = 512
TM_MIX = 512
FCUM_ROWS = 64
TQ = 256
ATTN_MAX_SCORE_BOUND = 100.0
HGRN_C = 64
HGRN_FAST_C = 128
HGRN_FAST_UNROLL = 4
HGRN_FAST_MAX_EXPONENT = 70.0
VMEM_LIMIT = 56 * 1024 * 1024


def _sigmoid(x):
    return 1.0 / (1.0 + jnp.exp(-x))


def _silu(x):
    return x * _sigmoid(x)


def _modulated_rms(x, g, shift, scale):
    ms = jnp.mean(x * x, axis=-1, keepdims=True)
    y = x * lax.rsqrt(ms + EPS) * g
    return y * (1.0 + scale) + shift


def _adaln_kernel(c_ref, w_ref, b_ref, o_ref):
    cs = _silu(c_ref[...]).astype(BF16)
    w = w_ref[...].astype(BF16)
    o_ref[...] = jnp.dot(cs, w, preferred_element_type=F32) + b_ref[...]


def _adaln(c, w, b):
    n = w.shape[1]
    tn = D_MODEL
    return pl.pallas_call(
        _adaln_kernel,
        grid=(n // tn,),
        in_specs=[
            pl.BlockSpec((BATCH, D_MODEL), lambda j: (0, 0)),
            pl.BlockSpec((D_MODEL, tn), lambda j: (0, j)),
            pl.BlockSpec((1, tn), lambda j: (0, j)),
        ],
        out_specs=pl.BlockSpec((BATCH, tn), lambda j: (0, j)),
        out_shape=jax.ShapeDtypeStruct((BATCH, n), F32),
        compiler_params=pltpu.CompilerParams(dimension_semantics=("arbitrary",)),
        name="adaln",
    )(c, w, b)


def _ffn_kernel(*refs, mod_base, with_mixer_out):
    if with_mixer_out:
        x_ref, mod_ref, g_ref, win_ref, wout_ref, oa_ref, oh_ref, ag_ref, wmo_ref, o_ref = refs
        oa = oa_ref[...].astype(F32)
        ms = jnp.mean(oa * oa, axis=-1, keepdims=True)
        oa = (oa * lax.rsqrt(ms + EPS) * ag_ref[...]).astype(BF16)
        cat = jnp.concatenate([oa, oh_ref[...]], axis=-1)
        mixed = jnp.dot(cat, wmo_ref[...], preferred_element_type=F32)
        x = x_ref[...] + mod_ref[0, 5:6, :] * mixed
    else:
        x_ref, mod_ref, g_ref, win_ref, wout_ref, o_ref = refs
        x = x_ref[...]
    shift = mod_ref[0, mod_base:mod_base + 1, :]
    scale = mod_ref[0, mod_base + 1:mod_base + 2, :]
    gate = mod_ref[0, mod_base + 2:mod_base + 3, :]
    h = _modulated_rms(x, g_ref[...], shift, scale).astype(BF16)
    gu = jnp.dot(h, win_ref[...], preferred_element_type=F32)
    a = (_silu(gu[:, :D_FF]) * gu[:, D_FF:]).astype(BF16)
    out = jnp.dot(a, wout_ref[...], preferred_element_type=F32)
    o_ref[...] = x + (FFN_RES * gate) * out


def _resident(shape):
    return pl.BlockSpec(shape, lambda i: (0,) * len(shape), pipeline_mode=pl.Buffered(1))


def _ffn(x2d, mod, g, w_in, w_out, mod_base, name, mixer_out=None):
    t = x2d.shape[0]
    tiles_per_seq = SEQ // TM_FFN
    row_blk = lambda n: pl.BlockSpec((TM_FFN, n), lambda i: (i, 0))
    in_specs = [
        row_blk(D_MODEL),
        pl.BlockSpec((1, N_MOD, D_MODEL), lambda i: (i // tiles_per_seq, 0, 0)),
        _resident((1, D_MODEL)),
        _resident((D_MODEL, 2 * D_FF)),
        _resident((D_FF, D_MODEL)),
    ]
    args = [x2d, mod, g, w_in, w_out]
    if mixer_out is not None:
        in_specs += [row_blk(D_ATTN), row_blk(D_HGRN), _resident((1, D_ATTN)),
                     _resident((D_MODEL, D_MODEL))]
        args += list(mixer_out)
    return pl.pallas_call(
        functools.partial(_ffn_kernel, mod_base=mod_base, with_mixer_out=mixer_out is not None),
        grid=(t // TM_FFN,),
        in_specs=in_specs,
        out_specs=row_blk(D_MODEL),
        out_shape=jax.ShapeDtypeStruct((t, D_MODEL), F32),
        compiler_params=pltpu.CompilerParams(
            dimension_semantics=("arbitrary",), vmem_limit_bytes=VMEM_LIMIT),
        name=name,
    )(*args)


def _mix_in_kernel(x_ref, mod_ref, g_ref, w_ref, wfg_ref, bfg_ref, qg_ref, kg_ref, lbl_ref,
                   qa_ref, ka_ref, va_ref, ls_ref, qf_ref, lf_ref, kk_ref, iv_ref, gt_ref):
    x = x_ref[...]
    h = _modulated_rms(x, g_ref[...], mod_ref[0, 3:4, :], mod_ref[0, 4:5, :]).astype(BF16)
    p = jnp.dot(h, w_ref[...], preferred_element_type=F32)

    lo = lax.broadcasted_iota(jnp.int32, (1, LANES), 1) < ATTN_HEAD_DIM
    inv_dh = 1.0 / ATTN_HEAD_DIM

    def head_norm(t, gain):
        sq = t * t
        s_lo = jnp.sum(jnp.where(lo, sq, 0.0), axis=-1, keepdims=True)
        s_hi = jnp.sum(jnp.where(lo, 0.0, sq), axis=-1, keepdims=True)
        r = jnp.where(lo, lax.rsqrt(s_lo * inv_dh + EPS), lax.rsqrt(s_hi * inv_dh + EPS))
        return (t * r * gain).astype(BF16)

    q_gain = qg_ref[...] * (ATTN_HEAD_DIM ** -0.5 * LOG2_E)
    k_gain = kg_ref[...]
    for j in range(D_ATTN // LANES):
        sl = slice(j * LANES, (j + 1) * LANES)
        qa_ref[:, sl] = head_norm(p[:, j * LANES:(j + 1) * LANES], q_gain)
        ka_ref[:, sl] = head_norm(p[:, D_ATTN + j * LANES:D_ATTN + (j + 1) * LANES], k_gain)
    va_ref[...] = p[:, 2 * D_ATTN:3 * D_ATTN].astype(BF16)

    fg = jnp.dot(h, wfg_ref[...], preferred_element_type=F32)
    fgt = fg.T[:N_ATTN_HEADS, :] + bfg_ref[:, 0:1]
    ls_ref[...] = jnp.minimum(fgt, 0.0) - jnp.log1p(jnp.exp(-jnp.abs(fgt)))

    lbl = lbl_ref[...]
    e = jnp.exp(lbl - jnp.max(lbl, axis=0, keepdims=True))
    lb = e[0:1, :] / jnp.sum(e, axis=0, keepdims=True)
    o0 = 3 * D_ATTN
    qh = p[:, o0:o0 + D_HGRN]
    z = p[:, o0 + D_HGRN:o0 + 2 * D_HGRN]
    ih = p[:, o0 + 2 * D_HGRN:o0 + 3 * D_HGRN]
    gh = p[:, o0 + 3 * D_HGRN:o0 + 4 * D_HGRN]
    t = jnp.exp(-jnp.abs(z))
    r = 1.0 / (1.0 + t)
    tr = t * r
    pos = z >= 0.0
    sig_p = jnp.where(pos, r, tr)
    sig_n = jnp.where(pos, tr, r)
    lf_ref[...] = jnp.log(lb + (1.0 - lb) * sig_p)
    kk_ref[...] = ((1.0 - lb) * sig_n).astype(BF16)
    qf_ref[...] = _silu(qh).astype(BF16)
    iv_ref[...] = ih.astype(BF16)
    gt_ref[...] = _silu(gh).astype(BF16)


def _mix_in(x2d, mod, g, w_main, w_fg, b_fg, q_g2, k_g2, lb_logits):
    t = x2d.shape[0]
    tm = TM_MIX
    tiles_per_seq = SEQ // tm
    row_blk = lambda n: pl.BlockSpec((tm, n), lambda i: (i, 0))
    bf = lambda n: jax.ShapeDtypeStruct((t, n), BF16)
    return pl.pallas_call(
        _mix_in_kernel,
        grid=(t // tm,),
        in_specs=[
            row_blk(D_MODEL),
            pl.BlockSpec((1, N_MOD, D_MODEL), lambda i: (i // tiles_per_seq, 0, 0)),
            _resident((1, D_MODEL)),
            _resident((D_MODEL, N_MAIN)),
            _resident((D_MODEL, LANES)),
            _resident((N_ATTN_HEADS, LANES)),
            _resident((1, LANES)),
            _resident((1, LANES)),
            _resident((2, D_HGRN)),
        ],
        out_specs=[
            row_blk(D_ATTN), row_blk(D_ATTN), row_blk(D_ATTN),
            pl.BlockSpec((None, N_ATTN_HEADS, tm),
                         lambda i: (i // tiles_per_seq, 0, i % tiles_per_seq)),
            row_blk(D_HGRN), row_blk(D_HGRN), row_blk(D_HGRN), row_blk(D_HGRN), row_blk(D_HGRN),
        ],
        out_shape=[
            bf(D_ATTN), bf(D_ATTN), bf(D_ATTN),
            jax.ShapeDtypeStruct((BATCH, N_ATTN_HEADS, SEQ), F32),
            bf(D_HGRN), jax.ShapeDtypeStruct((t, D_HGRN), F32), bf(D_HGRN), bf(D_HGRN), bf(D_HGRN),
        ],
        compiler_params=pltpu.CompilerParams(
            dimension_semantics=("arbitrary",), vmem_limit_bytes=VMEM_LIMIT),
        name="mix_in",
    )(x2d, mod, g, w_main, w_fg, b_fg, q_g2, k_g2, lb_logits)


def _fcum_kernel(ls_ref, nf_ref):
    blk = 256
    r = lax.broadcasted_iota(jnp.int32, (blk, blk), 0)
    c = lax.broadcasted_iota(jnp.int32, (blk, blk), 1)
    upper = (r <= c).astype(F32)
    carry = jnp.zeros((FCUM_ROWS, 1), F32)
    for j in range(SEQ // blk):
        x = ls_ref[:, j * blk:(j + 1) * blk]
        cs = jnp.dot(x, upper, precision=lax.Precision.HIGHEST,
                     preferred_element_type=F32) + carry
        nf_ref[:, j * blk:(j + 1) * blk] = cs * (-LOG2_E)
        carry = cs[:, blk - 1:blk]


def _fcum(ls):
    spec = pl.BlockSpec((FCUM_ROWS, SEQ), lambda i: (i, 0))
    return pl.pallas_call(
        _fcum_kernel,
        grid=(ls.shape[0] // FCUM_ROWS,),
        in_specs=[spec],
        out_specs=spec,
        out_shape=jax.ShapeDtypeStruct(ls.shape, F32),
        compiler_params=pltpu.CompilerParams(dimension_semantics=("arbitrary",)),
        name="fcum",
    )(ls)


def _mixer_kernel(q_ref, k_ref, v_ref, nf_ref, qg_ref, kg_ref,
                  qf_ref, kk_ref, iv_ref, lf_ref, gt_ref, hg_ref, lbl_ref,
                  o_ref, oh_ref, st_ref):
    hp = pl.program_id(1)
    lane = lax.broadcasted_iota(jnp.int32, (1, LANES), 1)
    lo = lane < ATTN_HEAD_DIM
    r_i = lax.broadcasted_iota(jnp.int32, (TQ, TQ), 0)
    c_i = lax.broadcasted_iota(jnp.int32, (TQ, TQ), 1)
    causal = c_i <= r_i
    nt = (((1,), (1,)), ((), ()))
    v1 = jnp.concatenate([v_ref[0], jnp.ones((SEQ, LANES), BF16)], axis=1)

    def attend(bounded, after_block=None):
        if bounded:
            nf8 = nf_ref[0]
            nf_t = jnp.concatenate(
                [nf8, jnp.zeros((LANES - N_ATTN_HEADS, SEQ), F32)], axis=0).T
        for qi in range(SEQ // TQ):
            q0 = qi * TQ
            q = q_ref[0, q0:q0 + TQ, :]
            k_d = k_ref[0, q0:q0 + TQ, :]
            v_d = v1[q0:q0 + TQ, :]
            outs = []
            for a in range(2):
                head = 2 * hp + a
                qa = jnp.where(lo if a == 0 else jnp.logical_not(lo), q, jnp.zeros_like(q))
                nf = nf_ref[0, pl.ds(head, 1), :]
                s_d = lax.dot_general(qa, k_d, nt, preferred_element_type=F32) + nf[:, q0:q0 + TQ]
                s_d = jnp.where(causal, s_d, -jnp.inf)
                if qi > 0:
                    s_f = lax.dot_general(qa, k_ref[0, :q0, :], nt,
                                          preferred_element_type=F32) + nf[:, :q0]
                if bounded:
                    m = jnp.sum(jnp.where(lane == head, nf_t[q0:q0 + TQ, :], 0.0),
                                axis=-1, keepdims=True)
                else:
                    m = jnp.max(s_d, axis=-1, keepdims=True)
                    if qi > 0:
                        m = jnp.maximum(m, jnp.max(s_f, axis=-1, keepdims=True))
                acc = jnp.dot(jnp.exp2(s_d - m).astype(BF16), v_d, preferred_element_type=F32)
                if qi > 0:
                    acc = acc + jnp.dot(jnp.exp2(s_f - m).astype(BF16), v1[:q0, :],
                                        preferred_element_type=F32)
                outs.append(acc[:, :LANES] * (1.0 / acc[:, LANES:]))
                if after_block is not None:
                    after_block(2 * qi + a)
            o_ref[0, q0:q0 + TQ, :] = jnp.where(lo, outs[0], outs[1]).astype(o_ref.dtype)

    bound = (ATTN_HEAD_DIM * ATTN_HEAD_DIM ** -0.5 * LOG2_E * 1.02) * (
        jnp.max(jnp.abs(qg_ref[...])) * jnp.max(jnp.abs(kg_ref[...])))
    bounded = bound <= ATTN_MAX_SCORE_BOUND
    fast, (stage1, stage2, stage3), fast_chunk, robust_chunk = _hgrn_ops(
        qf_ref, kk_ref, iv_ref, lf_ref, gt_ref, hg_ref, lbl_ref, oh_ref)
    interleave = jnp.logical_and(bounded, fast)
    n_chunks = SEQ // HGRN_FAST_C
    assert n_chunks == 2 * (SEQ // TQ)

    @pl.when(interleave)
    def _():
        state = [jnp.zeros((HGRN_K, HGRN_K), F32)]
        cumsum, staged = {}, {}

        def hgrn_stages(i):
            if 0 <= i - 2 < n_chunks:
                state[0] = stage3((i - 2) * HGRN_FAST_C, staged.pop(i - 2), state[0])
            if 0 <= i - 1 < n_chunks:
                staged[i - 1] = stage2((i - 1) * HGRN_FAST_C, cumsum.pop(i - 1))
            if i < n_chunks and i % 2 == 0:
                cumsum[i], cumsum[i + 1] = stage1(i * HGRN_FAST_C, 2)

        attend(True, after_block=hgrn_stages)
        hgrn_stages(n_chunks)
        hgrn_stages(n_chunks + 1)

    @pl.when(jnp.logical_not(interleave))
    def _():
        @pl.when(bounded)
        def _():
            attend(True)

        @pl.when(jnp.logical_not(bounded))
        def _():
            attend(False)

        st_ref[...] = jnp.zeros(st_ref.shape, F32)

        def looped(chunk_fn, c_len):
            def body(ci, carry):
                st_ref[...] = chunk_fn(pl.multiple_of(ci * c_len, c_len), st_ref[...])
                return carry
            lax.fori_loop(0, SEQ // c_len, body, 0)

        @pl.when(fast)
        def _():
            looped(fast_chunk, HGRN_FAST_C)

        @pl.when(jnp.logical_not(fast))
        def _():
            looped(robust_chunk, HGRN_C)


def _mixer(qa, ka, va, nf, q_g2, k_g2, qf, kk, iv, lf, gt, hgrn_g, lb_logits):
    head_spec = pl.BlockSpec((1, SEQ, LANES), lambda b, h: (b, 0, h))
    gain_spec = pl.BlockSpec((1, LANES), lambda b, h: (0, 0))
    out = jax.ShapeDtypeStruct((BATCH, SEQ, D_ATTN), BF16)
    return pl.pallas_call(
        _mixer_kernel,
        grid=(BATCH, D_ATTN // LANES),
        in_specs=[
            head_spec, head_spec, head_spec,
            pl.BlockSpec((1, N_ATTN_HEADS, SEQ), lambda b, h: (b, 0, 0)),
            gain_spec, gain_spec,
            head_spec, head_spec, head_spec, head_spec, head_spec,
            pl.BlockSpec((1, HGRN_K), lambda b, h: (0, h)),
            pl.BlockSpec((2, HGRN_K), lambda b, h: (0, h)),
        ],
        out_specs=[head_spec, head_spec],
        out_shape=[out, out],
        scratch_shapes=[pltpu.VMEM((HGRN_K, HGRN_K), F32)],
        compiler_params=pltpu.CompilerParams(
            dimension_semantics=("arbitrary", "arbitrary"), vmem_limit_bytes=VMEM_LIMIT),
        name="mixer",
    )(qa, ka, va, nf, q_g2, k_g2, qf, kk, iv, lf, gt, hgrn_g, lb_logits)


def _hgrn_ops(qf_ref, kk_ref, iv_ref, lf_ref, gt_ref, g_ref, lbl_ref, o_ref):
    c_len = HGRN_C
    nt = (((1,), (1,)), ((), ()))
    tn = (((0,), (0,)), ((), ()))

    def finish(o, gate, t0, rows):
        ms = jnp.mean(o * o, axis=-1, keepdims=True)
        y = o * lax.rsqrt(ms + EPS) * g_ref[...]
        o_ref[0, pl.ds(t0, rows), :] = (y * gate.astype(F32)).astype(o_ref.dtype)

    fc = HGRN_FAST_C
    fr = lax.broadcasted_iota(jnp.int32, (fc, fc), 0)
    fcol = lax.broadcasted_iota(jnp.int32, (fc, fc), 1)
    f_tril = fr >= fcol
    f_tril_bf = f_tril.astype(BF16)

    def fast_stage1(t0, n):
        lf = jnp.concatenate(
            [lf_ref[0, pl.ds(t0 + u * fc, fc), :] for u in range(n)], axis=1)
        hi = lf.astype(BF16)
        r1 = lf - hi.astype(F32)
        mid = r1.astype(BF16)
        lo = (r1 - mid.astype(F32)).astype(BF16)
        b = (jnp.dot(f_tril_bf, hi, preferred_element_type=F32)
             + jnp.dot(f_tril_bf, mid, preferred_element_type=F32)
             + jnp.dot(f_tril_bf, lo, preferred_element_type=F32))
        return [b[:, u * HGRN_K:(u + 1) * HGRN_K] for u in range(n)]

    def fast_stage2(t0, b):
        qf = qf_ref[0, pl.ds(t0, fc), :].astype(F32)
        kk = kk_ref[0, pl.ds(t0, fc), :].astype(F32)
        iv_t = iv_ref[0, pl.ds(t0, fc), :].astype(F32).T.astype(BF16)
        b_mid = b[fc // 2 - 1:fc // 2, :]
        b_last = b[fc - 1:fc, :]
        d = b - b_mid
        qt = qf * jnp.exp(d)
        kt = kk * jnp.exp(-d)
        qi = (qt * jnp.exp(b_mid)).astype(BF16)
        kd = (kt * jnp.exp(b_last - b_mid)).astype(BF16)
        a = lax.dot_general(qt.astype(BF16), kt.astype(BF16), nt, preferred_element_type=F32)
        st_inc = jnp.dot(iv_t, kd, preferred_element_type=F32)
        return a, qi, iv_t, st_inc, jnp.exp(b_last)

    def fast_stage3(t0, staged, st):
        a, qi, iv_t, st_inc, st_decay = staged
        a = jnp.where(f_tril, a, 0.0).astype(BF16)
        o = lax.dot_general(jnp.concatenate([a, qi], axis=1),
                            jnp.concatenate([iv_t, st.astype(BF16)], axis=1),
                            nt, preferred_element_type=F32)
        finish(o, gt_ref[0, pl.ds(t0, fc), :], t0, fc)
        return st * st_decay + st_inc

    def fast_chunk(t0, st):
        return fast_stage3(t0, fast_stage2(t0, fast_stage1(t0, 1)[0]), st)


    row = lax.broadcasted_iota(jnp.int32, (c_len, LANES), 0)
    r_i = lax.broadcasted_iota(jnp.int32, (c_len, c_len), 0)
    c_i = lax.broadcasted_iota(jnp.int32, (c_len, c_len), 1)
    halves = []
    hh = SUBLANES
    while hh < c_len:
        halves.append(hh)
        hh *= 2
    lvl_masks = [
        ((r_i // (2 * h)) == (c_i // (2 * h))) & ((r_i % (2 * h)) >= h) & ((c_i % (2 * h)) < h)
        for h in halves
    ]
    diag_masks = [
        (c_i == (r_i // SUBLANES) * SUBLANES + j) & ((r_i % SUBLANES) >= j)
        for j in range(SUBLANES)
    ]
    def rows_bcast(x, group, idx):
        if group == c_len:
            return jnp.broadcast_to(x[idx:idx + 1, :], x.shape)
        x3 = x.reshape(c_len // group, group, LANES)
        return jnp.broadcast_to(x3[:, idx:idx + 1, :], x3.shape).reshape(c_len, LANES)

    def robust_chunk(t0, st):
        lf = lf_ref[0, pl.ds(t0, c_len), :]
        qf = qf_ref[0, pl.ds(t0, c_len), :].astype(F32)
        kk = kk_ref[0, pl.ds(t0, c_len), :].astype(F32)
        iv = iv_ref[0, pl.ds(t0, c_len), :]
        b = lf
        k = 1
        while k < c_len:
            b = b + jnp.where(row >= k, pltpu.roll(b, k, 0), 0.0)
            k *= 2
        b_last = b[c_len - 1:c_len, :]
        qi = (qf * jnp.exp(b)).astype(BF16)
        o = lax.dot_general(qi, st.astype(BF16), nt, preferred_element_type=F32)
        a = jnp.zeros((c_len, c_len), F32)
        for h, msk in zip(halves, lvl_masks):
            ref = rows_bcast(b, 2 * h, h - 1)
            ql = (qf * jnp.exp(jnp.minimum(b - ref, 0.0))).astype(BF16)
            kl = (kk * jnp.exp(jnp.minimum(ref - b, 0.0))).astype(BF16)
            al = lax.dot_general(ql, kl, nt, preferred_element_type=F32)
            a = jnp.where(msk, al, a)
        for j in range(SUBLANES):
            bs = rows_bcast(b, SUBLANES, j)
            ks = rows_bcast(kk, SUBLANES, j)
            e = jnp.exp(jnp.minimum(b - bs, 0.0))
            col = jnp.sum(qf * ks * e, axis=-1, keepdims=True)
            a = jnp.where(diag_masks[j], col, a)
        o = o + jnp.dot(a.astype(BF16), iv, preferred_element_type=F32)
        finish(o, gt_ref[0, pl.ds(t0, c_len), :], t0, c_len)
        kd = (kk * jnp.exp(b_last - b)).astype(BF16)
        return st * jnp.exp(b_last) + lax.dot_general(iv, kd, tn, preferred_element_type=F32)

    lbl = lbl_ref[...]
    e = jnp.exp(lbl - jnp.max(lbl, axis=0, keepdims=True))
    lb = e[0:1, :] / jnp.sum(e, axis=0, keepdims=True)
    worst_exponent = (HGRN_FAST_C // 2) * jnp.max(-jnp.log(lb))
    fast = worst_exponent <= HGRN_FAST_MAX_EXPONENT
    return fast, (fast_stage1, fast_stage2, fast_stage3), fast_chunk, robust_chunk


def kernel(x, c, w_ada, b_ada, g_norm1, ffn1_w_in, ffn1_w_out, g_norm_mix, w_in_mix, b_fgate,
           q_norm_g, k_norm_g, attn_out_g, hgrn_lb_logits, hgrn_out_g, w_out_mix, g_norm2,
           ffn2_w_in, ffn2_w_out):
    assert x.shape == (BATCH, SEQ, D_MODEL) and w_ada.shape[0] == 1
    t = BATCH * SEQ
    mod = _adaln(c, w_ada[0], b_ada).reshape(BATCH, N_MOD, D_MODEL)

    x0 = x.reshape(t, D_MODEL)
    x1 = _ffn(x0, mod, g_norm1, ffn1_w_in[0].astype(BF16), ffn1_w_out[0].astype(BF16), 0, "ffn1")

    w_mix = w_in_mix[0]
    fg0 = 3 * D_ATTN
    w_main = jnp.concatenate(
        [w_mix[:, :fg0].astype(BF16), w_mix[:, fg0 + N_ATTN_HEADS:].astype(BF16)], axis=1)
    w_fg = jnp.pad(w_mix[:, fg0:fg0 + N_ATTN_HEADS], ((0, 0), (0, LANES - N_ATTN_HEADS))).astype(BF16)
    b_fg = jnp.broadcast_to(b_fgate[0][:, None], (N_ATTN_HEADS, LANES))
    q_g2 = jnp.tile(q_norm_g, (1, LANES // ATTN_HEAD_DIM))
    k_g2 = jnp.tile(k_norm_g, (1, LANES // ATTN_HEAD_DIM))
    qa, ka, va, ls, qf, lf, kk, iv, gt = _mix_in(
        x1, mod, g_norm_mix, w_main, w_fg, b_fg, q_g2, k_g2, hgrn_lb_logits)

    nf = _fcum(ls.reshape(BATCH * N_ATTN_HEADS, SEQ)).reshape(BATCH, N_ATTN_HEADS, SEQ)
    seq3 = lambda a: a.reshape(BATCH, SEQ, a.shape[-1])
    oa, oh = _mixer(seq3(qa), seq3(ka), seq3(va), nf, q_g2, k_g2,
                    seq3(qf), seq3(kk), seq3(iv), seq3(lf), seq3(gt), hgrn_out_g, hgrn_lb_logits)

    mixer_out = (oa.reshape(t, D_ATTN), oh.reshape(t, D_HGRN), attn_out_g, w_out_mix[0].astype(BF16))
    x3 = _ffn(x1, mod, g_norm2, ffn2_w_in[0].astype(BF16), ffn2_w_out[0].astype(BF16), 6, "ffn2",
              mixer_out=mixer_out)
    return x3.reshape(BATCH, SEQ, D_MODEL)
```

```python
import functools

import jax
import jax.numpy as jnp
from jax import lax
from jax.experimental import pallas as pl
from jax.experimental.pallas import tpu as pltpu

D_MODEL = 1024
BATCH = 32
SEQ = 2048
D_ATTN = 512
D_HGRN = 512
ATTN_HEAD_DIM = 64
N_ATTN_HEADS = 8
HGRN_K = 128
N_HGRN_HEADS = 4
D_FF = 2816
N_MOD = 9
FFN_RES = 0.5
EPS = 1e-6
LOG2_E = 1.4426950408889634

LANES = 128
SUBLANES = 8
N_MAIN = 3 * D_ATTN + 4 * D_HGRN

F32 = jnp.float32
BF16 = jnp.bfloat16

TM_FFN = 1024
FFN_SUB = 512
TM_MIX = 1024
MIX_SUB = 512
FCUM_ROWS = 64
TQ = 256
ATTN_MAX_SCORE_BOUND = 100.0
HGRN_C = 64
HGRN_FAST_C = 128
HGRN_FAST_UNROLL = 4
HGRN_FAST_MAX_EXPONENT = 70.0
VMEM_LIMIT = 56 * 1024 * 1024


def _sigmoid(x):
    return 1.0 / (1.0 + jnp.exp(-x))


def _silu(x):
    return x * _sigmoid(x)


def _modulated_rms(x, g, shift, scale):
    ms = jnp.mean(x * x, axis=-1, keepdims=True)
    y = x * lax.rsqrt(ms + EPS) * g
    return y * (1.0 + scale) + shift


def _adaln_kernel(c_ref, w_ref, b_ref, o_ref):
    cs = _silu(c_ref[...]).astype(BF16)
    w = w_ref[...].astype(BF16)
    o_ref[...] = jnp.dot(cs, w, preferred_element_type=F32) + b_ref[...]


def _adaln(c, w, b):
    n = w.shape[1]
    tn = D_MODEL
    return pl.pallas_call(
        _adaln_kernel,
        grid=(n // tn,),
        in_specs=[
            pl.BlockSpec((BATCH, D_MODEL), lambda j: (0, 0)),
            pl.BlockSpec((D_MODEL, tn), lambda j: (0, j)),
            pl.BlockSpec((1, tn), lambda j: (0, j)),
        ],
        out_specs=pl.BlockSpec((BATCH, tn), lambda j: (0, j)),
        out_shape=jax.ShapeDtypeStruct((BATCH, n), F32),
        compiler_params=pltpu.CompilerParams(dimension_semantics=("arbitrary",)),
        name="adaln",
    )(c, w, b)


def _ffn_kernel(*refs, mod_base, with_mixer_out):
    if with_mixer_out:
        x_ref, mod_ref, g_ref, win_ref, wout_ref, oa_ref, oh_ref, ag_ref, wmo_ref, o_ref = refs
    else:
        x_ref, mod_ref, g_ref, win_ref, wout_ref, o_ref = refs
    shift = mod_ref[0, mod_base:mod_base + 1, :]
    scale = mod_ref[0, mod_base + 1:mod_base + 2, :]
    gate = mod_ref[0, mod_base + 2:mod_base + 3, :]
    for r in range(TM_FFN // FFN_SUB):
        rows = slice(r * FFN_SUB, (r + 1) * FFN_SUB)
        x = x_ref[rows, :]
        if with_mixer_out:
            oa = oa_ref[rows, :].astype(F32)
            ms = jnp.mean(oa * oa, axis=-1, keepdims=True)
            oa = (oa * lax.rsqrt(ms + EPS) * ag_ref[...]).astype(BF16)
            cat = jnp.concatenate([oa, oh_ref[rows, :]], axis=-1)
            mixed = jnp.dot(cat, wmo_ref[...], preferred_element_type=F32)
            x = x + mod_ref[0, 5:6, :] * mixed
        h = _modulated_rms(x, g_ref[...], shift, scale).astype(BF16)
        gu = jnp.dot(h, win_ref[...], preferred_element_type=F32)
        a = (_silu(gu[:, :D_FF]) * gu[:, D_FF:]).astype(BF16)
        out = jnp.dot(a, wout_ref[...], preferred_element_type=F32)
        o_ref[rows, :] = x + (FFN_RES * gate) * out


def _resident(shape):
    return pl.BlockSpec(shape, lambda i: (0,) * len(shape), pipeline_mode=pl.Buffered(1))


def _ffn(x2d, mod, g, w_in, w_out, mod_base, name, mixer_out=None):
    t = x2d.shape[0]
    tiles_per_seq = SEQ // TM_FFN
    row_blk = lambda n: pl.BlockSpec((TM_FFN, n), lambda i: (i, 0))
    in_specs = [
        row_blk(D_MODEL),
        pl.BlockSpec((1, N_MOD, D_MODEL), lambda i: (i // tiles_per_seq, 0, 0)),
        _resident((1, D_MODEL)),
        _resident((D_MODEL, 2 * D_FF)),
        _resident((D_FF, D_MODEL)),
    ]
    args = [x2d, mod, g, w_in, w_out]
    if mixer_out is not None:
        in_specs += [row_blk(D_ATTN), row_blk(D_HGRN), _resident((1, D_ATTN)),
                     _resident((D_MODEL, D_MODEL))]
        args += list(mixer_out)
    return pl.pallas_call(
        functools.partial(_ffn_kernel, mod_base=mod_base, with_mixer_out=mixer_out is not None),
        grid=(t // TM_FFN,),
        in_specs=in_specs,
        out_specs=row_blk(D_MODEL),
        out_shape=jax.ShapeDtypeStruct((t, D_MODEL), F32),
        compiler_params=pltpu.CompilerParams(
            dimension_semantics=("arbitrary",), vmem_limit_bytes=VMEM_LIMIT),
        name=name,
    )(*args)


def _mix_in_kernel(x_ref, mod_ref, g_ref, w_ref, wfg_ref, bfg_ref, qg_ref, kg_ref, lbl_ref,
                   qa_ref, ka_ref, va_ref, ls_ref, *hgrn_refs):
    for r in range(TM_MIX // MIX_SUB):
        rows = pl.ds(r * MIX_SUB, MIX_SUB)
        _mix_in_rows(x_ref.at[rows, :], mod_ref, g_ref, w_ref, wfg_ref, bfg_ref, qg_ref, kg_ref,
                     lbl_ref, qa_ref.at[rows, :], ka_ref.at[rows, :], va_ref.at[rows, :],
                     ls_ref.at[:, rows], *[ref.at[rows, :] for ref in hgrn_refs])


def _mix_in_rows(x_ref, mod_ref, g_ref, w_ref, wfg_ref, bfg_ref, qg_ref, kg_ref, lbl_ref,
                 qa_ref, ka_ref, va_ref, ls_ref, qf_ref, lf_ref, kk_ref, iv_ref, gt_ref):
    x = x_ref[...]
    h = _modulated_rms(x, g_ref[...], mod_ref[0, 3:4, :], mod_ref[0, 4:5, :]).astype(BF16)
    p = jnp.dot(h, w_ref[...], preferred_element_type=F32)

    lo = lax.broadcasted_iota(jnp.int32, (1, LANES), 1) < ATTN_HEAD_DIM
    inv_dh = 1.0 / ATTN_HEAD_DIM

    def head_norm(t, gain):
        sq = t * t
        s_lo = jnp.sum(jnp.where(lo, sq, 0.0), axis=-1, keepdims=True)
        s_hi = jnp.sum(jnp.where(lo, 0.0, sq), axis=-1, keepdims=True)
        r = jnp.where(lo, lax.rsqrt(s_lo * inv_dh + EPS), lax.rsqrt(s_hi * inv_dh + EPS))
        return (t * r * gain).astype(BF16)

    q_gain = qg_ref[...] * (ATTN_HEAD_DIM ** -0.5 * LOG2_E)
    k_gain = kg_ref[...]
    for j in range(D_ATTN // LANES):
        sl = slice(j * LANES, (j + 1) * LANES)
        qa_ref[:, sl] = head_norm(p[:, j * LANES:(j + 1) * LANES], q_gain)
        ka_ref[:, sl] = head_norm(p[:, D_ATTN + j * LANES:D_ATTN + (j + 1) * LANES], k_gain)
    va_ref[...] = p[:, 2 * D_ATTN:3 * D_ATTN].astype(BF16)

    fg = jnp.dot(h, wfg_ref[...], preferred_element_type=F32)
    fgt = fg.T[:N_ATTN_HEADS, :] + bfg_ref[:, 0:1]
    ls_ref[...] = jnp.minimum(fgt, 0.0) - jnp.log1p(jnp.exp(-jnp.abs(fgt)))

    lbl = lbl_ref[...]
    e = jnp.exp(lbl - jnp.max(lbl, axis=0, keepdims=True))
    lb = e[0:1, :] / jnp.sum(e, axis=0, keepdims=True)
    o0 = 3 * D_ATTN
    qh = p[:, o0:o0 + D_HGRN]
    z = p[:, o0 + D_HGRN:o0 + 2 * D_HGRN]
    ih = p[:, o0 + 2 * D_HGRN:o0 + 3 * D_HGRN]
    gh = p[:, o0 + 3 * D_HGRN:o0 + 4 * D_HGRN]
    t = jnp.exp(-jnp.abs(z))
    r = 1.0 / (1.0 + t)
    tr = t * r
    pos = z >= 0.0
    sig_p = jnp.where(pos, r, tr)
    sig_n = jnp.where(pos, tr, r)
    lf_ref[...] = jnp.log(lb + (1.0 - lb) * sig_p)
    kk_ref[...] = ((1.0 - lb) * sig_n).astype(BF16)
    qf_ref[...] = _silu(qh).astype(BF16)
    iv_ref[...] = ih.astype(BF16)
    gt_ref[...] = _silu(gh).astype(BF16)


def _mix_in(x2d, mod, g, w_main, w_fg, b_fg, q_g2, k_g2, lb_logits):
    t = x2d.shape[0]
    tm = TM_MIX
    tiles_per_seq = SEQ // tm
    row_blk = lambda n: pl.BlockSpec((tm, n), lambda i: (i, 0))
    bf = lambda n: jax.ShapeDtypeStruct((t, n), BF16)
    return pl.pallas_call(
        _mix_in_kernel,
        grid=(t // tm,),
        in_specs=[
            row_blk(D_MODEL),
            pl.BlockSpec((1, N_MOD, D_MODEL), lambda i: (i // tiles_per_seq, 0, 0)),
            _resident((1, D_MODEL)),
            _resident((D_MODEL, N_MAIN)),
            _resident((D_MODEL, LANES)),
            _resident((N_ATTN_HEADS, LANES)),
            _resident((1, LANES)),
            _resident((1, LANES)),
            _resident((2, D_HGRN)),
        ],
        out_specs=[
            row_blk(D_ATTN), row_blk(D_ATTN), row_blk(D_ATTN),
            pl.BlockSpec((None, N_ATTN_HEADS, tm),
                         lambda i: (i // tiles_per_seq, 0, i % tiles_per_seq)),
            row_blk(D_HGRN), row_blk(D_HGRN), row_blk(D_HGRN), row_blk(D_HGRN), row_blk(D_HGRN),
        ],
        out_shape=[
            bf(D_ATTN), bf(D_ATTN), bf(D_ATTN),
            jax.ShapeDtypeStruct((BATCH, N_ATTN_HEADS, SEQ), F32),
            bf(D_HGRN), jax.ShapeDtypeStruct((t, D_HGRN), F32), bf(D_HGRN), bf(D_HGRN), bf(D_HGRN),
        ],
        compiler_params=pltpu.CompilerParams(
            dimension_semantics=("arbitrary",), vmem_limit_bytes=VMEM_LIMIT),
        name="mix_in",
    )(x2d, mod, g, w_main, w_fg, b_fg, q_g2, k_g2, lb_logits)


def _fcum_kernel(ls_ref, nf_ref):
    blk = 256
    r = lax.broadcasted_iota(jnp.int32, (blk, blk), 0)
    c = lax.broadcasted_iota(jnp.int32, (blk, blk), 1)
    upper = (r <= c).astype(F32)
    carry = jnp.zeros((FCUM_ROWS, 1), F32)
    for j in range(SEQ // blk):
        x = ls_ref[:, j * blk:(j + 1) * blk]
        cs = jnp.dot(x, upper, precision=lax.Precision.HIGHEST,
                     preferred_element_type=F32) + carry
        nf_ref[:, j * blk:(j + 1) * blk] = cs * (-LOG2_E)
        carry = cs[:, blk - 1:blk]


def _fcum(ls):
    spec = pl.BlockSpec((FCUM_ROWS, SEQ), lambda i: (i, 0))
    return pl.pallas_call(
        _fcum_kernel,
        grid=(ls.shape[0] // FCUM_ROWS,),
        in_specs=[spec],
        out_specs=spec,
        out_shape=jax.ShapeDtypeStruct(ls.shape, F32),
        compiler_params=pltpu.CompilerParams(dimension_semantics=("arbitrary",)),
        name="fcum",
    )(ls)


def _mixer_kernel(q_ref, k_ref, v_ref, nf_ref, qg_ref, kg_ref,
                  qf_ref, kk_ref, iv_ref, lf_ref, gt_ref, hg_ref, lbl_ref,
                  o_ref, oh_ref, st_ref):
    hp = pl.program_id(1)
    lane = lax.broadcasted_iota(jnp.int32, (1, LANES), 1)
    lo = lane < ATTN_HEAD_DIM
    r_i = lax.broadcasted_iota(jnp.int32, (TQ, TQ), 0)
    c_i = lax.broadcasted_iota(jnp.int32, (TQ, TQ), 1)
    causal = c_i <= r_i
    nt = (((1,), (1,)), ((), ()))
    v1 = jnp.concatenate([v_ref[0], jnp.ones((SEQ, LANES), BF16)], axis=1)

    def attend(bounded, after_block=None):
        if bounded:
            nf8 = nf_ref[0]
            nf_t = jnp.concatenate(
                [nf8, jnp.zeros((LANES - N_ATTN_HEADS, SEQ), F32)], axis=0).T
        for qi in range(SEQ // TQ):
            q0 = qi * TQ
            q = q_ref[0, q0:q0 + TQ, :]
            k_d = k_ref[0, q0:q0 + TQ, :]
            v_d = v1[q0:q0 + TQ, :]
            outs = []
            for a in range(2):
                head = 2 * hp + a
                qa = jnp.where(lo if a == 0 else jnp.logical_not(lo), q, jnp.zeros_like(q))
                nf = nf_ref[0, pl.ds(head, 1), :]
                s_d = lax.dot_general(qa, k_d, nt, preferred_element_type=F32) + nf[:, q0:q0 + TQ]
                s_d = jnp.where(causal, s_d, -jnp.inf)
                if qi > 0:
                    s_f = lax.dot_general(qa, k_ref[0, :q0, :], nt,
                                          preferred_element_type=F32) + nf[:, :q0]
                if bounded:
                    m = jnp.sum(jnp.where(lane == head, nf_t[q0:q0 + TQ, :], 0.0),
                                axis=-1, keepdims=True)
                else:
                    m = jnp.max(s_d, axis=-1, keepdims=True)
                    if qi > 0:
                        m = jnp.maximum(m, jnp.max(s_f, axis=-1, keepdims=True))
                acc = jnp.dot(jnp.exp2(s_d - m).astype(BF16), v_d, preferred_element_type=F32)
                if qi > 0:
                    acc = acc + jnp.dot(jnp.exp2(s_f - m).astype(BF16), v1[:q0, :],
                                        preferred_element_type=F32)
                outs.append(acc[:, :LANES] * (1.0 / acc[:, LANES:]))
                if after_block is not None:
                    after_block(2 * qi + a)
            o_ref[0, q0:q0 + TQ, :] = jnp.where(lo, outs[0], outs[1]).astype(o_ref.dtype)

    bound = (ATTN_HEAD_DIM * ATTN_HEAD_DIM ** -0.5 * LOG2_E * 1.02) * (
        jnp.max(jnp.abs(qg_ref[...])) * jnp.max(jnp.abs(kg_ref[...])))
    bounded = bound <= ATTN_MAX_SCORE_BOUND
    fast, (stage1, stage2, stage3), fast_chunk, robust_chunk = _hgrn_ops(
        qf_ref, kk_ref, iv_ref, lf_ref, gt_ref, hg_ref, lbl_ref, oh_ref)
    interleave = jnp.logical_and(bounded, fast)
    n_chunks = SEQ // HGRN_FAST_C
    assert n_chunks == 2 * (SEQ // TQ)

    @pl.when(interleave)
    def _():
        state = [jnp.zeros((HGRN_K, HGRN_K), F32)]
        cumsum, staged = {}, {}

        def hgrn_stages(i):
            if 0 <= i - 2 < n_chunks:
                state[0] = stage3((i - 2) * HGRN_FAST_C, staged.pop(i - 2), state[0])
            if 0 <= i - 1 < n_chunks:
                staged[i - 1] = stage2((i - 1) * HGRN_FAST_C, cumsum.pop(i - 1))
            if i < n_chunks and i % 2 == 0:
                cumsum[i], cumsum[i + 1] = stage1(i * HGRN_FAST_C, 2)

        attend(True, after_block=hgrn_stages)
        hgrn_stages(n_chunks)
        hgrn_stages(n_chunks + 1)

    @pl.when(jnp.logical_not(interleave))
    def _():
        @pl.when(bounded)
        def _():
            attend(True)

        @pl.when(jnp.logical_not(bounded))
        def _():
            attend(False)

        st_ref[...] = jnp.zeros(st_ref.shape, F32)

        def looped(chunk_fn, c_len):
            def body(ci, carry):
                st_ref[...] = chunk_fn(pl.multiple_of(ci * c_len, c_len), st_ref[...])
                return carry
            lax.fori_loop(0, SEQ // c_len, body, 0)

        @pl.when(fast)
        def _():
            looped(fast_chunk, HGRN_FAST_C)

        @pl.when(jnp.logical_not(fast))
        def _():
            looped(robust_chunk, HGRN_C)


def _mixer(qa, ka, va, nf, q_g2, k_g2, qf, kk, iv, lf, gt, hgrn_g, lb_logits):
    head_spec = pl.BlockSpec((1, SEQ, LANES), lambda b, h: (b, 0, h))
    gain_spec = pl.BlockSpec((1, LANES), lambda b, h: (0, 0))
    out = jax.ShapeDtypeStruct((BATCH, SEQ, D_ATTN), BF16)
    return pl.pallas_call(
        _mixer_kernel,
        grid=(BATCH, D_ATTN // LANES),
        in_specs=[
            head_spec, head_spec, head_spec,
            pl.BlockSpec((1, N_ATTN_HEADS, SEQ), lambda b, h: (b, 0, 0)),
            gain_spec, gain_spec,
            head_spec, head_spec, head_spec, head_spec, head_spec,
            pl.BlockSpec((1, HGRN_K), lambda b, h: (0, h)),
            pl.BlockSpec((2, HGRN_K), lambda b, h: (0, h)),
        ],
        out_specs=[head_spec, head_spec],
        out_shape=[out, out],
        scratch_shapes=[pltpu.VMEM((HGRN_K, HGRN_K), F32)],
        compiler_params=pltpu.CompilerParams(
            dimension_semantics=("arbitrary", "arbitrary"), vmem_limit_bytes=VMEM_LIMIT),
        name="mixer",
    )(qa, ka, va, nf, q_g2, k_g2, qf, kk, iv, lf, gt, hgrn_g, lb_logits)


def _hgrn_ops(qf_ref, kk_ref, iv_ref, lf_ref, gt_ref, g_ref, lbl_ref, o_ref):
    c_len = HGRN_C
    nt = (((1,), (1,)), ((), ()))
    tn = (((0,), (0,)), ((), ()))

    def finish(o, gate, t0, rows):
        ms = jnp.mean(o * o, axis=-1, keepdims=True)
        y = o * lax.rsqrt(ms + EPS) * g_ref[...]
        o_ref[0, pl.ds(t0, rows), :] = (y * gate.astype(F32)).astype(o_ref.dtype)

    fc = HGRN_FAST_C
    fr = lax.broadcasted_iota(jnp.int32, (fc, fc), 0)
    fcol = lax.broadcasted_iota(jnp.int32, (fc, fc), 1)
    f_tril = fr >= fcol
    f_tril_bf = f_tril.astype(BF16)

    def fast_stage1(t0, n):
        lf = jnp.concatenate(
            [lf_ref[0, pl.ds(t0 + u * fc, fc), :] for u in range(n)], axis=1)
        hi = lf.astype(BF16)
        r1 = lf - hi.astype(F32)
        mid = r1.astype(BF16)
        lo = (r1 - mid.astype(F32)).astype(BF16)
        b = (jnp.dot(f_tril_bf, hi, preferred_element_type=F32)
             + jnp.dot(f_tril_bf, mid, preferred_element_type=F32)
             + jnp.dot(f_tril_bf, lo, preferred_element_type=F32))
        return [b[:, u * HGRN_K:(u + 1) * HGRN_K] for u in range(n)]

    def fast_stage2(t0, b):
        qf = qf_ref[0, pl.ds(t0, fc), :].astype(F32)
        kk = kk_ref[0, pl.ds(t0, fc), :].astype(F32)
        iv_t = iv_ref[0, pl.ds(t0, fc), :].astype(F32).T.astype(BF16)
        b_mid = b[fc // 2 - 1:fc // 2, :]
        b_last = b[fc - 1:fc, :]
        d = b - b_mid
        qt = qf * jnp.exp(d)
        kt = kk * jnp.exp(-d)
        qi = (qt * jnp.exp(b_mid)).astype(BF16)
        kd = (kt * jnp.exp(b_last - b_mid)).astype(BF16)
        a = lax.dot_general(qt.astype(BF16), kt.astype(BF16), nt, preferred_element_type=F32)
        st_inc = jnp.dot(iv_t, kd, preferred_element_type=F32)
        return a, qi, iv_t, st_inc, jnp.exp(b_last)

    def fast_stage3(t0, staged, st):
        a, qi, iv_t, st_inc, st_decay = staged
        a = jnp.where(f_tril, a, 0.0).astype(BF16)
        o = lax.dot_general(jnp.concatenate([a, qi], axis=1),
                            jnp.concatenate([iv_t, st.astype(BF16)], axis=1),
                            nt, preferred_element_type=F32)
        finish(o, gt_ref[0, pl.ds(t0, fc), :], t0, fc)
        return st * st_decay + st_inc

    def fast_chunk(t0, st):
        return fast_stage3(t0, fast_stage2(t0, fast_stage1(t0, 1)[0]), st)


    row = lax.broadcasted_iota(jnp.int32, (c_len, LANES), 0)
    r_i = lax.broadcasted_iota(jnp.int32, (c_len, c_len), 0)
    c_i = lax.broadcasted_iota(jnp.int32, (c_len, c_len), 1)
    halves = []
    hh = SUBLANES
    while hh < c_len:
        halves.append(hh)
        hh *= 2
    lvl_masks = [
        ((r_i // (2 * h)) == (c_i // (2 * h))) & ((r_i % (2 * h)) >= h) & ((c_i % (2 * h)) < h)
        for h in halves
    ]
    diag_masks = [
        (c_i == (r_i // SUBLANES) * SUBLANES + j) & ((r_i % SUBLANES) >= j)
        for j in range(SUBLANES)
    ]
    def rows_bcast(x, group, idx):
        if group == c_len:
            return jnp.broadcast_to(x[idx:idx + 1, :], x.shape)
        x3 = x.reshape(c_len // group, group, LANES)
        return jnp.broadcast_to(x3[:, idx:idx + 1, :], x3.shape).reshape(c_len, LANES)

    def robust_chunk(t0, st):
        lf = lf_ref[0, pl.ds(t0, c_len), :]
        qf = qf_ref[0, pl.ds(t0, c_len), :].astype(F32)
        kk = kk_ref[0, pl.ds(t0, c_len), :].astype(F32)
        iv = iv_ref[0, pl.ds(t0, c_len), :]
        b = lf
        k = 1
        while k < c_len:
            b = b + jnp.where(row >= k, pltpu.roll(b, k, 0), 0.0)
            k *= 2
        b_last = b[c_len - 1:c_len, :]
        qi = (qf * jnp.exp(b)).astype(BF16)
        o = lax.dot_general(qi, st.astype(BF16), nt, preferred_element_type=F32)
        a = jnp.zeros((c_len, c_len), F32)
        for h, msk in zip(halves, lvl_masks):
            ref = rows_bcast(b, 2 * h, h - 1)
            ql = (qf * jnp.exp(jnp.minimum(b - ref, 0.0))).astype(BF16)
            kl = (kk * jnp.exp(jnp.minimum(ref - b, 0.0))).astype(BF16)
            al = lax.dot_general(ql, kl, nt, preferred_element_type=F32)
            a = jnp.where(msk, al, a)
        for j in range(SUBLANES):
            bs = rows_bcast(b, SUBLANES, j)
            ks = rows_bcast(kk, SUBLANES, j)
            e = jnp.exp(jnp.minimum(b - bs, 0.0))
            col = jnp.sum(qf * ks * e, axis=-1, keepdims=True)
            a = jnp.where(diag_masks[j], col, a)
        o = o + jnp.dot(a.astype(BF16), iv, preferred_element_type=F32)
        finish(o, gt_ref[0, pl.ds(t0, c_len), :], t0, c_len)
        kd = (kk * jnp.exp(b_last - b)).astype(BF16)
        return st * jnp.exp(b_last) + lax.dot_general(iv, kd, tn, preferred_element_type=F32)

    lbl = lbl_ref[...]
    e = jnp.exp(lbl - jnp.max(lbl, axis=0, keepdims=True))
    lb = e[0:1, :] / jnp.sum(e, axis=0, keepdims=True)
    worst_exponent = (HGRN_FAST_C // 2) * jnp.max(-jnp.log(lb))
    fast = worst_exponent <= HGRN_FAST_MAX_EXPONENT
    return fast, (fast_stage1, fast_stage2, fast_stage3), fast_chunk, robust_chunk


def kernel(x, c, w_ada, b_ada, g_norm1, ffn1_w_in, ffn1_w_out, g_norm_mix, w_in_mix, b_fgate,
           q_norm_g, k_norm_g, attn_out_g, hgrn_lb_logits, hgrn_out_g, w_out_mix, g_norm2,
           ffn2_w_in, ffn2_w_out):
    assert x.shape == (BATCH, SEQ, D_MODEL) and w_ada.shape[0] == 1
    t = BATCH * SEQ
    mod = _adaln(c, w_ada[0], b_ada).reshape(BATCH, N_MOD, D_MODEL)

    x0 = x.reshape(t, D_MODEL)
    x1 = _ffn(x0, mod, g_norm1, ffn1_w_in[0].astype(BF16), ffn1_w_out[0].astype(BF16), 0, "ffn1")

    w_mix = w_in_mix[0]
    fg0 = 3 * D_ATTN
    w_main = jnp.concatenate(
        [w_mix[:, :fg0].astype(BF16), w_mix[:, fg0 + N_ATTN_HEADS:].astype(BF16)], axis=1)
    w_fg = jnp.pad(w_mix[:, fg0:fg0 + N_ATTN_HEADS], ((0, 0), (0, LANES - N_ATTN_HEADS))).astype(BF16)
    b_fg = jnp.broadcast_to(b_fgate[0][:, None], (N_ATTN_HEADS, LANES))
    q_g2 = jnp.tile(q_norm_g, (1, LANES // ATTN_HEAD_DIM))
    k_g2 = jnp.tile(k_norm_g, (1, LANES // ATTN_HEAD_DIM))
    qa, ka, va, ls, qf, lf, kk, iv, gt = _mix_in(
        x1, mod, g_norm_mix, w_main, w_fg, b_fg, q_g2, k_g2, hgrn_lb_logits)

    nf = _fcum(ls.reshape(BATCH * N_ATTN_HEADS, SEQ)).reshape(BATCH, N_ATTN_HEADS, SEQ)
    seq3 = lambda a: a.reshape(BATCH, SEQ, a.shape[-1])
    oa, oh = _mixer(seq3(qa), seq3(ka), seq3(va), nf, q_g2, k_g2,
                    seq3(qf), seq3(kk), seq3(iv), seq3(lf), seq3(gt), hgrn_out_g, hgrn_lb_logits)

    mixer_out = (oa.reshape(t, D_ATTN), oh.reshape(t, D_HGRN), attn_out_g, w_out_mix[0].astype(BF16))
    x3 = _ffn(x1, mod, g_norm2, ffn2_w_in[0].astype(BF16), ffn2_w_out[0].astype(BF16), 6, "ffn2",
              mixer_out=mixer_out)
    return x3.reshape(BATCH, SEQ, D_MODEL)
```

```python
import functools

import jax
import jax.numpy as jnp
from jax import lax
from jax.experimental import pallas as pl
from jax.experimental.pallas import tpu as pltpu

D_MODEL = 1024
BATCH = 32
SEQ = 2048
D_ATTN = 512
D_HGRN = 512
ATTN_HEAD_DIM = 64
N_ATTN_HEADS = 8
HGRN_K = 128
N_HGRN_HEADS = 4
D_FF = 2816
N_MOD = 9
FFN_RES = 0.5
EPS = 1e-6
LOG2_E = 1.4426950408889634

LANES = 128
SUBLANES = 8
N_MAIN = 3 * D_ATTN + 4 * D_HGRN

F32 = jnp.float32
BF16 = jnp.bfloat16

TM_FFN = 1024
FFN_SUB = 512
TM_MIX = 1024
MIX_SUB = 512
FCUM_ROWS = 64
TQ = 256
ATTN_BLOCK_ORDER = (7, 6, 5, 4, 3, 2, 1, 0)
ATTN_MAX_SCORE_BOUND = 100.0
HGRN_C = 64
HGRN_FAST_C = 128
HGRN_FAST_UNROLL = 4
HGRN_FAST_MAX_EXPONENT = 70.0
VMEM_LIMIT = 56 * 1024 * 1024


def _sigmoid(x):
    return 1.0 / (1.0 + jnp.exp(-x))


def _silu(x):
    return x * _sigmoid(x)


def _modulated_rms(x, g, shift, scale):
    ms = jnp.mean(x * x, axis=-1, keepdims=True)
    y = x * lax.rsqrt(ms + EPS) * g
    return y * (1.0 + scale) + shift


def _adaln_kernel(c_ref, w_ref, b_ref, o_ref):
    cs = _silu(c_ref[...]).astype(BF16)
    w = w_ref[...].astype(BF16)
    o_ref[...] = jnp.dot(cs, w, preferred_element_type=F32) + b_ref[...]


def _adaln(c, w, b):
    n = w.shape[1]
    tn = D_MODEL
    return pl.pallas_call(
        _adaln_kernel,
        grid=(n // tn,),
        in_specs=[
            pl.BlockSpec((BATCH, D_MODEL), lambda j: (0, 0)),
            pl.BlockSpec((D_MODEL, tn), lambda j: (0, j)),
            pl.BlockSpec((1, tn), lambda j: (0, j)),
        ],
        out_specs=pl.BlockSpec((BATCH, tn), lambda j: (0, j)),
        out_shape=jax.ShapeDtypeStruct((BATCH, n), F32),
        compiler_params=pltpu.CompilerParams(dimension_semantics=("arbitrary",)),
        name="adaln",
    )(c, w, b)


def _ffn_kernel(*refs, mod_base, with_mixer_out):
    if with_mixer_out:
        x_ref, mod_ref, g_ref, win_ref, wout_ref, oa_ref, oh_ref, ag_ref, wmo_ref, o_ref = refs
    else:
        x_ref, mod_ref, g_ref, win_ref, wout_ref, o_ref = refs
    shift = mod_ref[0, mod_base:mod_base + 1, :]
    scale = mod_ref[0, mod_base + 1:mod_base + 2, :]
    gate = mod_ref[0, mod_base + 2:mod_base + 3, :]
    for r in range(TM_FFN // FFN_SUB):
        rows = slice(r * FFN_SUB, (r + 1) * FFN_SUB)
        x = x_ref[rows, :]
        if with_mixer_out:
            oa = oa_ref[rows, :].astype(F32)
            ms = jnp.mean(oa * oa, axis=-1, keepdims=True)
            oa = (oa * lax.rsqrt(ms + EPS) * ag_ref[...]).astype(BF16)
            cat = jnp.concatenate([oa, oh_ref[rows, :]], axis=-1)
            mixed = jnp.dot(cat, wmo_ref[...], preferred_element_type=F32)
            x = x + mod_ref[0, 5:6, :] * mixed
        h = _modulated_rms(x, g_ref[...], shift, scale).astype(BF16)
        gu = jnp.dot(h, win_ref[...], preferred_element_type=F32)
        a = (_silu(gu[:, :D_FF]) * gu[:, D_FF:]).astype(BF16)
        out = jnp.dot(a, wout_ref[...], preferred_element_type=F32)
        o_ref[rows, :] = x + (FFN_RES * gate) * out


def _resident(shape):
    return pl.BlockSpec(shape, lambda i: (0,) * len(shape), pipeline_mode=pl.Buffered(1))


def _ffn(x2d, mod, g, w_in, w_out, mod_base, name, mixer_out=None):
    t = x2d.shape[0]
    tiles_per_seq = SEQ // TM_FFN
    row_blk = lambda n: pl.BlockSpec((TM_FFN, n), lambda i: (i, 0))
    in_specs = [
        row_blk(D_MODEL),
        pl.BlockSpec((1, N_MOD, D_MODEL), lambda i: (i // tiles_per_seq, 0, 0)),
        _resident((1, D_MODEL)),
        _resident((D_MODEL, 2 * D_FF)),
        _resident((D_FF, D_MODEL)),
    ]
    args = [x2d, mod, g, w_in, w_out]
    if mixer_out is not None:
        in_specs += [row_blk(D_ATTN), row_blk(D_HGRN), _resident((1, D_ATTN)),
                     _resident((D_MODEL, D_MODEL))]
        args += list(mixer_out)
    return pl.pallas_call(
        functools.partial(_ffn_kernel, mod_base=mod_base, with_mixer_out=mixer_out is not None),
        grid=(t // TM_FFN,),
        in_specs=in_specs,
        out_specs=row_blk(D_MODEL),
        out_shape=jax.ShapeDtypeStruct((t, D_MODEL), F32),
        compiler_params=pltpu.CompilerParams(
            dimension_semantics=("arbitrary",), vmem_limit_bytes=VMEM_LIMIT),
        name=name,
    )(*args)


def _mix_in_kernel(x_ref, mod_ref, g_ref, w_ref, wfg_ref, bfg_ref, qg_ref, kg_ref, lbl_ref,
                   qa_ref, ka_ref, va_ref, ls_ref, *hgrn_refs):
    for r in range(TM_MIX // MIX_SUB):
        rows = pl.ds(r * MIX_SUB, MIX_SUB)
        _mix_in_rows(x_ref.at[rows, :], mod_ref, g_ref, w_ref, wfg_ref, bfg_ref, qg_ref, kg_ref,
                     lbl_ref, qa_ref.at[rows, :], ka_ref.at[rows, :], va_ref.at[rows, :],
                     ls_ref.at[:, rows], *[ref.at[rows, :] for ref in hgrn_refs])


def _mix_in_rows(x_ref, mod_ref, g_ref, w_ref, wfg_ref, bfg_ref, qg_ref, kg_ref, lbl_ref,
                 qa_ref, ka_ref, va_ref, ls_ref, qf_ref, lf_ref, kk_ref, iv_ref, gt_ref):
    x = x_ref[...]
    h = _modulated_rms(x, g_ref[...], mod_ref[0, 3:4, :], mod_ref[0, 4:5, :]).astype(BF16)
    p = jnp.dot(h, w_ref[...], preferred_element_type=F32)

    lo = lax.broadcasted_iota(jnp.int32, (1, LANES), 1) < ATTN_HEAD_DIM
    inv_dh = 1.0 / ATTN_HEAD_DIM

    def head_norm(t, gain):
        sq = t * t
        s_lo = jnp.sum(jnp.where(lo, sq, 0.0), axis=-1, keepdims=True)
        s_hi = jnp.sum(jnp.where(lo, 0.0, sq), axis=-1, keepdims=True)
        r = jnp.where(lo, lax.rsqrt(s_lo * inv_dh + EPS), lax.rsqrt(s_hi * inv_dh + EPS))
        return (t * r * gain).astype(BF16)

    q_gain = qg_ref[...] * (ATTN_HEAD_DIM ** -0.5 * LOG2_E)
    k_gain = kg_ref[...]
    for j in range(D_ATTN // LANES):
        sl = slice(j * LANES, (j + 1) * LANES)
        qa_ref[:, sl] = head_norm(p[:, j * LANES:(j + 1) * LANES], q_gain)
        ka_ref[:, sl] = head_norm(p[:, D_ATTN + j * LANES:D_ATTN + (j + 1) * LANES], k_gain)
    va_ref[...] = p[:, 2 * D_ATTN:3 * D_ATTN].astype(BF16)

    fg = jnp.dot(h, wfg_ref[...], preferred_element_type=F32)
    fgt = fg.T[:N_ATTN_HEADS, :] + bfg_ref[:, 0:1]
    ls_ref[...] = jnp.minimum(fgt, 0.0) - jnp.log1p(jnp.exp(-jnp.abs(fgt)))

    lbl = lbl_ref[...]
    e = jnp.exp(lbl - jnp.max(lbl, axis=0, keepdims=True))
    lb = e[0:1, :] / jnp.sum(e, axis=0, keepdims=True)
    o0 = 3 * D_ATTN
    qh = p[:, o0:o0 + D_HGRN]
    z = p[:, o0 + D_HGRN:o0 + 2 * D_HGRN]
    ih = p[:, o0 + 2 * D_HGRN:o0 + 3 * D_HGRN]
    gh = p[:, o0 + 3 * D_HGRN:o0 + 4 * D_HGRN]
    t = jnp.exp(-jnp.abs(z))
    r = 1.0 / (1.0 + t)
    tr = t * r
    pos = z >= 0.0
    sig_p = jnp.where(pos, r, tr)
    sig_n = jnp.where(pos, tr, r)
    lf_ref[...] = jnp.log(lb + (1.0 - lb) * sig_p)
    kk_ref[...] = ((1.0 - lb) * sig_n).astype(BF16)
    qf_ref[...] = _silu(qh).astype(BF16)
    iv_ref[...] = ih.astype(BF16)
    gt_ref[...] = _silu(gh).astype(BF16)


def _mix_in(x2d, mod, g, w_main, w_fg, b_fg, q_g2, k_g2, lb_logits):
    t = x2d.shape[0]
    tm = TM_MIX
    tiles_per_seq = SEQ // tm
    row_blk = lambda n: pl.BlockSpec((tm, n), lambda i: (i, 0))
    bf = lambda n: jax.ShapeDtypeStruct((t, n), BF16)
    return pl.pallas_call(
        _mix_in_kernel,
        grid=(t // tm,),
        in_specs=[
            row_blk(D_MODEL),
            pl.BlockSpec((1, N_MOD, D_MODEL), lambda i: (i // tiles_per_seq, 0, 0)),
            _resident((1, D_MODEL)),
            _resident((D_MODEL, N_MAIN)),
            _resident((D_MODEL, LANES)),
            _resident((N_ATTN_HEADS, LANES)),
            _resident((1, LANES)),
            _resident((1, LANES)),
            _resident((2, D_HGRN)),
        ],
        out_specs=[
            row_blk(D_ATTN), row_blk(D_ATTN), row_blk(D_ATTN),
            pl.BlockSpec((None, N_ATTN_HEADS, tm),
                         lambda i: (i // tiles_per_seq, 0, i % tiles_per_seq)),
            row_blk(D_HGRN), row_blk(D_HGRN), row_blk(D_HGRN), row_blk(D_HGRN), row_blk(D_HGRN),
        ],
        out_shape=[
            bf(D_ATTN), bf(D_ATTN), bf(D_ATTN),
            jax.ShapeDtypeStruct((BATCH, N_ATTN_HEADS, SEQ), F32),
            bf(D_HGRN), jax.ShapeDtypeStruct((t, D_HGRN), F32), bf(D_HGRN), bf(D_HGRN), bf(D_HGRN),
        ],
        compiler_params=pltpu.CompilerParams(
            dimension_semantics=("arbitrary",), vmem_limit_bytes=VMEM_LIMIT),
        name="mix_in",
    )(x2d, mod, g, w_main, w_fg, b_fg, q_g2, k_g2, lb_logits)


def _fcum_kernel(ls_ref, nf_ref):
    blk = 256
    r = lax.broadcasted_iota(jnp.int32, (blk, blk), 0)
    c = lax.broadcasted_iota(jnp.int32, (blk, blk), 1)
    upper = (r <= c).astype(F32)
    carry = jnp.zeros((FCUM_ROWS, 1), F32)
    for j in range(SEQ // blk):
        x = ls_ref[:, j * blk:(j + 1) * blk]
        cs = jnp.dot(x, upper, precision=lax.Precision.HIGHEST,
                     preferred_element_type=F32) + carry
        nf_ref[:, j * blk:(j + 1) * blk] = cs * (-LOG2_E)
        carry = cs[:, blk - 1:blk]


def _fcum(ls):
    spec = pl.BlockSpec((FCUM_ROWS, SEQ), lambda i: (i, 0))
    return pl.pallas_call(
        _fcum_kernel,
        grid=(ls.shape[0] // FCUM_ROWS,),
        in_specs=[spec],
        out_specs=spec,
        out_shape=jax.ShapeDtypeStruct(ls.shape, F32),
        compiler_params=pltpu.CompilerParams(dimension_semantics=("arbitrary",)),
        name="fcum",
    )(ls)


def _mixer_kernel(q_ref, k_ref, v_ref, nf_ref, qg_ref, kg_ref,
                  qf_ref, kk_ref, iv_ref, lf_ref, gt_ref, hg_ref, lbl_ref,
                  o_ref, oh_ref, st_ref):
    hp = pl.program_id(1)
    lane = lax.broadcasted_iota(jnp.int32, (1, LANES), 1)
    lo = lane < ATTN_HEAD_DIM
    r_i = lax.broadcasted_iota(jnp.int32, (TQ, TQ), 0)
    c_i = lax.broadcasted_iota(jnp.int32, (TQ, TQ), 1)
    causal = c_i <= r_i
    nt = (((1,), (1,)), ((), ()))
    v1 = jnp.concatenate([v_ref[0], jnp.ones((SEQ, LANES), BF16)], axis=1)

    def attend(bounded, after_block=None):
        if bounded:
            nf8 = nf_ref[0]
            nf_t = jnp.concatenate(
                [nf8, jnp.zeros((LANES - N_ATTN_HEADS, SEQ), F32)], axis=0).T
        def scores(qi, a):
            q0 = qi * TQ
            head = 2 * hp + a
            q = q_ref[0, q0:q0 + TQ, :]
            qa = jnp.where(lo if a == 0 else jnp.logical_not(lo), q, jnp.zeros_like(q))
            nf = nf_ref[0, pl.ds(head, 1), :]
            s_d = lax.dot_general(qa, k_ref[0, q0:q0 + TQ, :], nt,
                                  preferred_element_type=F32) + nf[:, q0:q0 + TQ]
            s_d = jnp.where(causal, s_d, -jnp.inf)
            s_f = None
            if qi > 0:
                s_f = lax.dot_general(qa, k_ref[0, :q0, :], nt,
                                      preferred_element_type=F32) + nf[:, :q0]
            return s_d, s_f

        def values(qi, a, s_d, s_f):
            q0 = qi * TQ
            if bounded:
                m = jnp.sum(jnp.where(lane == 2 * hp + a, nf_t[q0:q0 + TQ, :], 0.0),
                            axis=-1, keepdims=True)
            else:
                m = jnp.max(s_d, axis=-1, keepdims=True)
                if qi > 0:
                    m = jnp.maximum(m, jnp.max(s_f, axis=-1, keepdims=True))
            acc = jnp.dot(jnp.exp2(s_d - m).astype(BF16), v1[q0:q0 + TQ, :],
                          preferred_element_type=F32)
            if qi > 0:
                acc = acc + jnp.dot(jnp.exp2(s_f - m).astype(BF16), v1[:q0, :],
                                    preferred_element_type=F32)
            return acc[:, :LANES] * (1.0 / acc[:, LANES:])

        blocks = [(qi, a) for qi in ATTN_BLOCK_ORDER for a in range(2)]
        pending = scores(*blocks[0])
        outs = []
        for i, (qi, a) in enumerate(blocks):
            current = pending
            if i + 1 < len(blocks):
                pending = scores(*blocks[i + 1])
            outs.append(values(qi, a, *current))
            if after_block is not None:
                after_block(i)
            if a == 1:
                o_ref[0, qi * TQ:(qi + 1) * TQ, :] = jnp.where(
                    lo, outs[0], outs[1]).astype(o_ref.dtype)
                outs = []

    bound = (ATTN_HEAD_DIM * ATTN_HEAD_DIM ** -0.5 * LOG2_E * 1.02) * (
        jnp.max(jnp.abs(qg_ref[...])) * jnp.max(jnp.abs(kg_ref[...])))
    bounded = bound <= ATTN_MAX_SCORE_BOUND
    fast, (stage1, stage2, stage3), fast_chunk, robust_chunk = _hgrn_ops(
        qf_ref, kk_ref, iv_ref, lf_ref, gt_ref, hg_ref, lbl_ref, oh_ref)
    interleave = jnp.logical_and(bounded, fast)
    n_chunks = SEQ // HGRN_FAST_C
    assert n_chunks == 2 * (SEQ // TQ)

    @pl.when(interleave)
    def _():
        state = [jnp.zeros((HGRN_K, HGRN_K), F32)]
        cumsum, staged = {}, {}

        def hgrn_stages(i):
            if 0 <= i - 2 < n_chunks:
                state[0] = stage3((i - 2) * HGRN_FAST_C, staged.pop(i - 2), state[0])
            if 0 <= i - 1 < n_chunks:
                staged[i - 1] = stage2((i - 1) * HGRN_FAST_C, cumsum.pop(i - 1))
            if i < n_chunks and i % 2 == 0:
                cumsum[i], cumsum[i + 1] = stage1(i * HGRN_FAST_C, 2)

        attend(True, after_block=hgrn_stages)
        hgrn_stages(n_chunks)
        hgrn_stages(n_chunks + 1)

    @pl.when(jnp.logical_not(interleave))
    def _():
        @pl.when(bounded)
        def _():
            attend(True)

        @pl.when(jnp.logical_not(bounded))
        def _():
            attend(False)

        st_ref[...] = jnp.zeros(st_ref.shape, F32)

        def looped(chunk_fn, c_len):
            def body(ci, carry):
                st_ref[...] = chunk_fn(pl.multiple_of(ci * c_len, c_len), st_ref[...])
                return carry
            lax.fori_loop(0, SEQ // c_len, body, 0)

        @pl.when(fast)
        def _():
            looped(fast_chunk, HGRN_FAST_C)

        @pl.when(jnp.logical_not(fast))
        def _():
            looped(robust_chunk, HGRN_C)


def _mixer(qa, ka, va, nf, q_g2, k_g2, qf, kk, iv, lf, gt, hgrn_g, lb_logits):
    head_spec = pl.BlockSpec((1, SEQ, LANES), lambda b, h: (b, 0, h))
    gain_spec = pl.BlockSpec((1, LANES), lambda b, h: (0, 0))
    out = jax.ShapeDtypeStruct((BATCH, SEQ, D_ATTN), BF16)
    return pl.pallas_call(
        _mixer_kernel,
        grid=(BATCH, D_ATTN // LANES),
        in_specs=[
            head_spec, head_spec, head_spec,
            pl.BlockSpec((1, N_ATTN_HEADS, SEQ), lambda b, h: (b, 0, 0)),
            gain_spec, gain_spec,
            head_spec, head_spec, head_spec, head_spec, head_spec,
            pl.BlockSpec((1, HGRN_K), lambda b, h: (0, h)),
            pl.BlockSpec((2, HGRN_K), lambda b, h: (0, h)),
        ],
        out_specs=[head_spec, head_spec],
        out_shape=[out, out],
        scratch_shapes=[pltpu.VMEM((HGRN_K, HGRN_K), F32)],
        compiler_params=pltpu.CompilerParams(
            dimension_semantics=("arbitrary", "arbitrary"), vmem_limit_bytes=VMEM_LIMIT),
        name="mixer",
    )(qa, ka, va, nf, q_g2, k_g2, qf, kk, iv, lf, gt, hgrn_g, lb_logits)


def _hgrn_ops(qf_ref, kk_ref, iv_ref, lf_ref, gt_ref, g_ref, lbl_ref, o_ref):
    c_len = HGRN_C
    nt = (((1,), (1,)), ((), ()))
    tn = (((0,), (0,)), ((), ()))

    def finish(o, gate, t0, rows):
        ms = jnp.mean(o * o, axis=-1, keepdims=True)
        y = o * lax.rsqrt(ms + EPS) * g_ref[...]
        o_ref[0, pl.ds(t0, rows), :] = (y * gate.astype(F32)).astype(o_ref.dtype)

    fc = HGRN_FAST_C
    fr = lax.broadcasted_iota(jnp.int32, (fc, fc), 0)
    fcol = lax.broadcasted_iota(jnp.int32, (fc, fc), 1)
    f_tril = fr >= fcol
    f_tril_bf = f_tril.astype(BF16)

    def fast_stage1(t0, n):
        lf = jnp.concatenate(
            [lf_ref[0, pl.ds(t0 + u * fc, fc), :] for u in range(n)], axis=1)
        hi = lf.astype(BF16)
        r1 = lf - hi.astype(F32)
        mid = r1.astype(BF16)
        lo = (r1 - mid.astype(F32)).astype(BF16)
        b = (jnp.dot(f_tril_bf, hi, preferred_element_type=F32)
             + jnp.dot(f_tril_bf, mid, preferred_element_type=F32)
             + jnp.dot(f_tril_bf, lo, preferred_element_type=F32))
        return [b[:, u * HGRN_K:(u + 1) * HGRN_K] for u in range(n)]

    def fast_stage2(t0, b):
        qf = qf_ref[0, pl.ds(t0, fc), :].astype(F32)
        kk = kk_ref[0, pl.ds(t0, fc), :].astype(F32)
        iv_t = iv_ref[0, pl.ds(t0, fc), :].astype(F32).T.astype(BF16)
        b_mid = b[fc // 2 - 1:fc // 2, :]
        b_last = b[fc - 1:fc, :]
        d = b - b_mid
        qt = qf * jnp.exp(d)
        kt = kk * jnp.exp(-d)
        qi = (qt * jnp.exp(b_mid)).astype(BF16)
        kd = (kt * jnp.exp(b_last - b_mid)).astype(BF16)
        a = lax.dot_general(qt.astype(BF16), kt.astype(BF16), nt, preferred_element_type=F32)
        st_inc = jnp.dot(iv_t, kd, preferred_element_type=F32)
        return a, qi, iv_t, st_inc, jnp.exp(b_last)

    def fast_stage3(t0, staged, st):
        a, qi, iv_t, st_inc, st_decay = staged
        a = jnp.where(f_tril, a, 0.0).astype(BF16)
        o = lax.dot_general(jnp.concatenate([a, qi], axis=1),
                            jnp.concatenate([iv_t, st.astype(BF16)], axis=1),
                            nt, preferred_element_type=F32)
        finish(o, gt_ref[0, pl.ds(t0, fc), :], t0, fc)
        return st * st_decay + st_inc

    def fast_chunk(t0, st):
        return fast_stage3(t0, fast_stage2(t0, fast_stage1(t0, 1)[0]), st)


    row = lax.broadcasted_iota(jnp.int32, (c_len, LANES), 0)
    r_i = lax.broadcasted_iota(jnp.int32, (c_len, c_len), 0)
    c_i = lax.broadcasted_iota(jnp.int32, (c_len, c_len), 1)
    halves = []
    hh = SUBLANES
    while hh < c_len:
        halves.append(hh)
        hh *= 2
    lvl_masks = [
        ((r_i // (2 * h)) == (c_i // (2 * h))) & ((r_i % (2 * h)) >= h) & ((c_i % (2 * h)) < h)
        for h in halves
    ]
    diag_masks = [
        (c_i == (r_i // SUBLANES) * SUBLANES + j) & ((r_i % SUBLANES) >= j)
        for j in range(SUBLANES)
    ]
    def rows_bcast(x, group, idx):
        if group == c_len:
            return jnp.broadcast_to(x[idx:idx + 1, :], x.shape)
        x3 = x.reshape(c_len // group, group, LANES)
        return jnp.broadcast_to(x3[:, idx:idx + 1, :], x3.shape).reshape(c_len, LANES)

    def robust_chunk(t0, st):
        lf = lf_ref[0, pl.ds(t0, c_len), :]
        qf = qf_ref[0, pl.ds(t0, c_len), :].astype(F32)
        kk = kk_ref[0, pl.ds(t0, c_len), :].astype(F32)
        iv = iv_ref[0, pl.ds(t0, c_len), :]
        b = lf
        k = 1
        while k < c_len:
            b = b + jnp.where(row >= k, pltpu.roll(b, k, 0), 0.0)
            k *= 2
        b_last = b[c_len - 1:c_len, :]
        qi = (qf * jnp.exp(b)).astype(BF16)
        o = lax.dot_general(qi, st.astype(BF16), nt, preferred_element_type=F32)
        a = jnp.zeros((c_len, c_len), F32)
        for h, msk in zip(halves, lvl_masks):
            ref = rows_bcast(b, 2 * h, h - 1)
            ql = (qf * jnp.exp(jnp.minimum(b - ref, 0.0))).astype(BF16)
            kl = (kk * jnp.exp(jnp.minimum(ref - b, 0.0))).astype(BF16)
            al = lax.dot_general(ql, kl, nt, preferred_element_type=F32)
            a = jnp.where(msk, al, a)
        for j in range(SUBLANES):
            bs = rows_bcast(b, SUBLANES, j)
            ks = rows_bcast(kk, SUBLANES, j)
            e = jnp.exp(jnp.minimum(b - bs, 0.0))
            col = jnp.sum(qf * ks * e, axis=-1, keepdims=True)
            a = jnp.where(diag_masks[j], col, a)
        o = o + jnp.dot(a.astype(BF16), iv, preferred_element_type=F32)
        finish(o, gt_ref[0, pl.ds(t0, c_len), :], t0, c_len)
        kd = (kk * jnp.exp(b_last - b)).astype(BF16)
        return st * jnp.exp(b_last) + lax.dot_general(iv, kd, tn, preferred_element_type=F32)

    lbl = lbl_ref[...]
    e = jnp.exp(lbl - jnp.max(lbl, axis=0, keepdims=True))
    lb = e[0:1, :] / jnp.sum(e, axis=0, keepdims=True)
    worst_exponent = (HGRN_FAST_C // 2) * jnp.max(-jnp.log(lb))
    fast = worst_exponent <= HGRN_FAST_MAX_EXPONENT
    return fast, (fast_stage1, fast_stage2, fast_stage3), fast_chunk, robust_chunk


def kernel(x, c, w_ada, b_ada, g_norm1, ffn1_w_in, ffn1_w_out, g_norm_mix, w_in_mix, b_fgate,
           q_norm_g, k_norm_g, attn_out_g, hgrn_lb_logits, hgrn_out_g, w_out_mix, g_norm2,
           ffn2_w_in, ffn2_w_out):
    assert x.shape == (BATCH, SEQ, D_MODEL) and w_ada.shape[0] == 1
    t = BATCH * SEQ
    mod = _adaln(c, w_ada[0], b_ada).reshape(BATCH, N_MOD, D_MODEL)

    x0 = x.reshape(t, D_MODEL)
    x1 = _ffn(x0, mod, g_norm1, ffn1_w_in[0].astype(BF16), ffn1_w_out[0].astype(BF16), 0, "ffn1")

    w_mix = w_in_mix[0]
    fg0 = 3 * D_ATTN
    w_main = jnp.concatenate(
        [w_mix[:, :fg0].astype(BF16), w_mix[:, fg0 + N_ATTN_HEADS:].astype(BF16)], axis=1)
    w_fg = jnp.pad(w_mix[:, fg0:fg0 + N_ATTN_HEADS], ((0, 0), (0, LANES - N_ATTN_HEADS))).astype(BF16)
    b_fg = jnp.broadcast_to(b_fgate[0][:, None], (N_ATTN_HEADS, LANES))
    q_g2 = jnp.tile(q_norm_g, (1, LANES // ATTN_HEAD_DIM))
    k_g2 = jnp.tile(k_norm_g, (1, LANES // ATTN_HEAD_DIM))
    qa, ka, va, ls, qf, lf, kk, iv, gt = _mix_in(
        x1, mod, g_norm_mix, w_main, w_fg, b_fg, q_g2, k_g2, hgrn_lb_logits)

    nf = _fcum(ls.reshape(BATCH * N_ATTN_HEADS, SEQ)).reshape(BATCH, N_ATTN_HEADS, SEQ)
    seq3 = lambda a: a.reshape(BATCH, SEQ, a.shape[-1])
    oa, oh = _mixer(seq3(qa), seq3(ka), seq3(va), nf, q_g2, k_g2,
                    seq3(qf), seq3(kk), seq3(iv), seq3(lf), seq3(gt), hgrn_out_g, hgrn_lb_logits)

    mixer_out = (oa.reshape(t, D_ATTN), oh.reshape(t, D_HGRN), attn_out_g, w_out_mix[0].astype(BF16))
    x3 = _ffn(x1, mod, g_norm2, ffn2_w_in[0].astype(BF16), ffn2_w_out[0].astype(BF16), 6, "ffn2",
              mixer_out=mixer_out)
    return x3.reshape(BATCH, SEQ, D_MODEL)
```

```python
import functools

import jax
import jax.numpy as jnp
from jax import lax
from jax.experimental import pallas as pl
from jax.experimental.pallas import tpu as pltpu

D_MODEL = 1024
BATCH = 32
SEQ = 2048
D_ATTN = 512
D_HGRN = 512
ATTN_HEAD_DIM = 64
N_ATTN_HEADS = 8
HGRN_K = 128
N_HGRN_HEADS = 4
D_FF = 2816
N_MOD = 9
FFN_RES = 0.5
EPS = 1e-6
LOG2_E = 1.4426950408889634

LANES = 128
SUBLANES = 8

F32 = jnp.float32
BF16 = jnp.bfloat16

TM_FFN = 1024
FFN_SUB = 512
TM_MIX = 1024
MIX_SUB = 512
FCUM_ROWS = 64
TQ = 256
ATTN_BLOCK_ORDER = (7, 6, 5, 4, 3, 2, 1, 0)
ATTN_MAX_SCORE_BOUND = 100.0
HGRN_C = 64
HGRN_FAST_C = 128
HGRN_FAST_UNROLL = 4
HGRN_FAST_MAX_EXPONENT = 70.0
VMEM_LIMIT = 56 * 1024 * 1024


def _sigmoid(x):
    return 1.0 / (1.0 + jnp.exp(-x))


def _silu(x):
    return x * _sigmoid(x)


def _modulated_rms(x, g, shift, scale):
    ms = jnp.mean(x * x, axis=-1, keepdims=True)
    y = x * lax.rsqrt(ms + EPS) * g
    return y * (1.0 + scale) + shift


def _adaln_kernel(c_ref, w_ref, b_ref, o_ref):
    cs = _silu(c_ref[...]).astype(BF16)
    w = w_ref[...].astype(BF16)
    o_ref[...] = jnp.dot(cs, w, preferred_element_type=F32) + b_ref[...]


def _adaln(c, w, b):
    n = w.shape[1]
    tn = D_MODEL
    return pl.pallas_call(
        _adaln_kernel,
        grid=(n // tn,),
        in_specs=[
            pl.BlockSpec((BATCH, D_MODEL), lambda j: (0, 0)),
            pl.BlockSpec((D_MODEL, tn), lambda j: (0, j)),
            pl.BlockSpec((1, tn), lambda j: (0, j)),
        ],
        out_specs=pl.BlockSpec((BATCH, tn), lambda j: (0, j)),
        out_shape=jax.ShapeDtypeStruct((BATCH, n), F32),
        compiler_params=pltpu.CompilerParams(dimension_semantics=("arbitrary",)),
        name="adaln",
    )(c, w, b)


def _ffn_kernel(*refs, mod_base, with_mixer_out):
    if with_mixer_out:
        x_ref, mod_ref, g_ref, win_ref, wout_ref, oa_ref, oh_ref, ag_ref, wmo_ref, o_ref = refs
    else:
        x_ref, mod_ref, g_ref, win_ref, wout_ref, o_ref = refs
    shift = mod_ref[0, mod_base:mod_base + 1, :]
    scale = mod_ref[0, mod_base + 1:mod_base + 2, :]
    gate = mod_ref[0, mod_base + 2:mod_base + 3, :]
    for r in range(TM_FFN // FFN_SUB):
        rows = slice(r * FFN_SUB, (r + 1) * FFN_SUB)
        x = x_ref[rows, :]
        if with_mixer_out:
            oa = oa_ref[rows, :].astype(F32)
            ms = jnp.mean(oa * oa, axis=-1, keepdims=True)
            oa = (oa * lax.rsqrt(ms + EPS) * ag_ref[...]).astype(BF16)
            cat = jnp.concatenate([oa, oh_ref[rows, :]], axis=-1)
            mixed = jnp.dot(cat, wmo_ref[...], preferred_element_type=F32)
            x = x + mod_ref[0, 5:6, :] * mixed
        h = _modulated_rms(x, g_ref[...], shift, scale).astype(BF16)
        gu = jnp.dot(h, win_ref[...], preferred_element_type=F32)
        a = (_silu(gu[:, :D_FF]) * gu[:, D_FF:]).astype(BF16)
        out = jnp.dot(a, wout_ref[...], preferred_element_type=F32)
        o_ref[rows, :] = x + (FFN_RES * gate) * out


def _resident(shape):
    return pl.BlockSpec(shape, lambda i: (0,) * len(shape), pipeline_mode=pl.Buffered(1))


def _ffn(x2d, mod, g, w_in, w_out, mod_base, name, mixer_out=None):
    t = x2d.shape[0]
    tiles_per_seq = SEQ // TM_FFN
    row_blk = lambda n: pl.BlockSpec((TM_FFN, n), lambda i: (i, 0))
    in_specs = [
        row_blk(D_MODEL),
        pl.BlockSpec((1, N_MOD, D_MODEL), lambda i: (i // tiles_per_seq, 0, 0)),
        _resident((1, D_MODEL)),
        _resident((D_MODEL, 2 * D_FF)),
        _resident((D_FF, D_MODEL)),
    ]
    args = [x2d, mod, g, w_in, w_out]
    if mixer_out is not None:
        in_specs += [row_blk(D_ATTN), row_blk(D_HGRN), _resident((1, D_ATTN)),
                     _resident((D_MODEL, D_MODEL))]
        args += list(mixer_out)
    return pl.pallas_call(
        functools.partial(_ffn_kernel, mod_base=mod_base, with_mixer_out=mixer_out is not None),
        grid=(t // TM_FFN,),
        in_specs=in_specs,
        out_specs=row_blk(D_MODEL),
        out_shape=jax.ShapeDtypeStruct((t, D_MODEL), F32),
        compiler_params=pltpu.CompilerParams(
            dimension_semantics=("arbitrary",), vmem_limit_bytes=VMEM_LIMIT),
        name=name,
    )(*args)


def _mix_in_kernel(x_ref, mod_ref, g_ref, wa_ref, wh_ref, wfg_ref, bfg_ref, qg_ref, kg_ref,
                   lbl_ref, qa_ref, ka_ref, va_ref, ls_ref, *hgrn_refs):
    for r in range(TM_MIX // MIX_SUB):
        rows = pl.ds(r * MIX_SUB, MIX_SUB)
        _mix_in_rows(x_ref.at[rows, :], mod_ref, g_ref, wa_ref, wh_ref, wfg_ref, bfg_ref, qg_ref,
                     kg_ref, lbl_ref, qa_ref.at[rows, :], ka_ref.at[rows, :], va_ref.at[rows, :],
                     ls_ref.at[:, rows], *[ref.at[rows, :] for ref in hgrn_refs])


def _mix_in_rows(x_ref, mod_ref, g_ref, wa_ref, wh_ref, wfg_ref, bfg_ref, qg_ref, kg_ref, lbl_ref,
                 qa_ref, ka_ref, va_ref, ls_ref, qf_ref, lf_ref, kk_ref, iv_ref, gt_ref):
    x = x_ref[...]
    h = _modulated_rms(x, g_ref[...], mod_ref[0, 3:4, :], mod_ref[0, 4:5, :]).astype(BF16)
    ph = jnp.dot(h, wh_ref[...], preferred_element_type=F32)
    p = jnp.dot(h, wa_ref[...], preferred_element_type=F32)

    lo = lax.broadcasted_iota(jnp.int32, (1, LANES), 1) < ATTN_HEAD_DIM
    inv_dh = 1.0 / ATTN_HEAD_DIM

    def head_norm(t, gain):
        sq = t * t
        s_lo = jnp.sum(jnp.where(lo, sq, 0.0), axis=-1, keepdims=True)
        s_hi = jnp.sum(jnp.where(lo, 0.0, sq), axis=-1, keepdims=True)
        r = jnp.where(lo, lax.rsqrt(s_lo * inv_dh + EPS), lax.rsqrt(s_hi * inv_dh + EPS))
        return (t * r * gain).astype(BF16)

    q_gain = qg_ref[...] * (ATTN_HEAD_DIM ** -0.5 * LOG2_E)
    k_gain = kg_ref[...]
    for j in range(D_ATTN // LANES):
        sl = slice(j * LANES, (j + 1) * LANES)
        qa_ref[:, sl] = head_norm(p[:, j * LANES:(j + 1) * LANES], q_gain)
        ka_ref[:, sl] = head_norm(p[:, D_ATTN + j * LANES:D_ATTN + (j + 1) * LANES], k_gain)
    va_ref[...] = p[:, 2 * D_ATTN:3 * D_ATTN].astype(BF16)

    fg = jnp.dot(h, wfg_ref[...], preferred_element_type=F32)
    fgt = fg.T[:N_ATTN_HEADS, :] + bfg_ref[:, 0:1]
    ls_ref[...] = jnp.minimum(fgt, 0.0) - jnp.log1p(jnp.exp(-jnp.abs(fgt)))

    lbl = lbl_ref[...]
    e = jnp.exp(lbl - jnp.max(lbl, axis=0, keepdims=True))
    lb = e[0:1, :] / jnp.sum(e, axis=0, keepdims=True)
    qh = ph[:, :D_HGRN]
    z = ph[:, D_HGRN:2 * D_HGRN]
    ih = ph[:, 2 * D_HGRN:3 * D_HGRN]
    gh = ph[:, 3 * D_HGRN:]
    t = jnp.exp(-jnp.abs(z))
    r = 1.0 / (1.0 + t)
    tr = t * r
    pos = z >= 0.0
    sig_p = jnp.where(pos, r, tr)
    sig_n = jnp.where(pos, tr, r)
    lf_ref[...] = jnp.log(lb + (1.0 - lb) * sig_p)
    kk_ref[...] = ((1.0 - lb) * sig_n).astype(BF16)
    qf_ref[...] = _silu(qh).astype(BF16)
    iv_ref[...] = ih.astype(BF16)
    gt_ref[...] = _silu(gh).astype(BF16)


def _mix_in(x2d, mod, g, w_att, w_hgrn, w_fg, b_fg, q_g2, k_g2, lb_logits):
    t = x2d.shape[0]
    tm = TM_MIX
    tiles_per_seq = SEQ // tm
    row_blk = lambda n: pl.BlockSpec((tm, n), lambda i: (i, 0))
    bf = lambda n: jax.ShapeDtypeStruct((t, n), BF16)
    return pl.pallas_call(
        _mix_in_kernel,
        grid=(t // tm,),
        in_specs=[
            row_blk(D_MODEL),
            pl.BlockSpec((1, N_MOD, D_MODEL), lambda i: (i // tiles_per_seq, 0, 0)),
            _resident((1, D_MODEL)),
            _resident((D_MODEL, 3 * D_ATTN)),
            _resident((D_MODEL, 4 * D_HGRN)),
            _resident((D_MODEL, LANES)),
            _resident((N_ATTN_HEADS, LANES)),
            _resident((1, LANES)),
            _resident((1, LANES)),
            _resident((2, D_HGRN)),
        ],
        out_specs=[
            row_blk(D_ATTN), row_blk(D_ATTN), row_blk(D_ATTN),
            pl.BlockSpec((None, N_ATTN_HEADS, tm),
                         lambda i: (i // tiles_per_seq, 0, i % tiles_per_seq)),
            row_blk(D_HGRN), row_blk(D_HGRN), row_blk(D_HGRN), row_blk(D_HGRN), row_blk(D_HGRN),
        ],
        out_shape=[
            bf(D_ATTN), bf(D_ATTN), bf(D_ATTN),
            jax.ShapeDtypeStruct((BATCH, N_ATTN_HEADS, SEQ), F32),
            bf(D_HGRN), jax.ShapeDtypeStruct((t, D_HGRN), F32), bf(D_HGRN), bf(D_HGRN), bf(D_HGRN),
        ],
        compiler_params=pltpu.CompilerParams(
            dimension_semantics=("arbitrary",), vmem_limit_bytes=VMEM_LIMIT),
        name="mix_in",
    )(x2d, mod, g, w_att, w_hgrn, w_fg, b_fg, q_g2, k_g2, lb_logits)


def _fcum_kernel(ls_ref, nf_ref):
    blk = 256
    r = lax.broadcasted_iota(jnp.int32, (blk, blk), 0)
    c = lax.broadcasted_iota(jnp.int32, (blk, blk), 1)
    upper = (r <= c).astype(F32)
    carry = jnp.zeros((FCUM_ROWS, 1), F32)
    for j in range(SEQ // blk):
        x = ls_ref[:, j * blk:(j + 1) * blk]
        cs = jnp.dot(x, upper, precision=lax.Precision.HIGHEST,
                     preferred_element_type=F32) + carry
        nf_ref[:, j * blk:(j + 1) * blk] = cs * (-LOG2_E)
        carry = cs[:, blk - 1:blk]


def _fcum(ls):
    spec = pl.BlockSpec((FCUM_ROWS, SEQ), lambda i: (i, 0))
    return pl.pallas_call(
        _fcum_kernel,
        grid=(ls.shape[0] // FCUM_ROWS,),
        in_specs=[spec],
        out_specs=spec,
        out_shape=jax.ShapeDtypeStruct(ls.shape, F32),
        compiler_params=pltpu.CompilerParams(dimension_semantics=("arbitrary",)),
        name="fcum",
    )(ls)


def _mixer_kernel(q_ref, k_ref, v_ref, nf_ref, qg_ref, kg_ref,
                  qf_ref, kk_ref, iv_ref, lf_ref, gt_ref, hg_ref, lbl_ref,
                  o_ref, oh_ref, st_ref, tril_ref):
    hp = pl.program_id(1)
    lane = lax.broadcasted_iota(jnp.int32, (1, LANES), 1)
    lo = lane < ATTN_HEAD_DIM
    r_i = lax.broadcasted_iota(jnp.int32, (TQ, TQ), 0)
    c_i = lax.broadcasted_iota(jnp.int32, (TQ, TQ), 1)
    causal = c_i <= r_i
    nt = (((1,), (1,)), ((), ()))
    v1 = jnp.concatenate([v_ref[0], jnp.ones((SEQ, LANES), BF16)], axis=1)

    def attend(bounded, after_block=None):
        if bounded:
            nf8 = nf_ref[0]
            nf_t = jnp.concatenate(
                [nf8, jnp.zeros((LANES - N_ATTN_HEADS, SEQ), F32)], axis=0).T
        def scores(qi, a):
            q0 = qi * TQ
            head = 2 * hp + a
            q = q_ref[0, q0:q0 + TQ, :]
            qa = jnp.where(lo if a == 0 else jnp.logical_not(lo), q, jnp.zeros_like(q))
            nf = nf_ref[0, pl.ds(head, 1), :]
            s_d = lax.dot_general(qa, k_ref[0, q0:q0 + TQ, :], nt,
                                  preferred_element_type=F32) + nf[:, q0:q0 + TQ]
            s_d = jnp.where(causal, s_d, -jnp.inf)
            s_f = None
            if qi > 0:
                s_f = lax.dot_general(qa, k_ref[0, :q0, :], nt,
                                      preferred_element_type=F32) + nf[:, :q0]
            return s_d, s_f

        def values(qi, a, s_d, s_f):
            q0 = qi * TQ
            if bounded:
                m = jnp.sum(jnp.where(lane == 2 * hp + a, nf_t[q0:q0 + TQ, :], 0.0),
                            axis=-1, keepdims=True)
            else:
                m = jnp.max(s_d, axis=-1, keepdims=True)
                if qi > 0:
                    m = jnp.maximum(m, jnp.max(s_f, axis=-1, keepdims=True))
            acc = jnp.dot(jnp.exp2(s_d - m).astype(BF16), v1[q0:q0 + TQ, :],
                          preferred_element_type=F32)
            if qi > 0:
                acc = acc + jnp.dot(jnp.exp2(s_f - m).astype(BF16), v1[:q0, :],
                                    preferred_element_type=F32)
            return acc[:, :LANES] * (1.0 / acc[:, LANES:])

        blocks = [(qi, a) for qi in ATTN_BLOCK_ORDER for a in range(2)]
        pending = scores(*blocks[0])
        outs = []
        for i, (qi, a) in enumerate(blocks):
            current = pending
            if i + 1 < len(blocks):
                pending = scores(*blocks[i + 1])
            outs.append(values(qi, a, *current))
            if after_block is not None:
                after_block(i)
            if a == 1:
                o_ref[0, qi * TQ:(qi + 1) * TQ, :] = jnp.where(
                    lo, outs[0], outs[1]).astype(o_ref.dtype)
                outs = []

    bound = (ATTN_HEAD_DIM * ATTN_HEAD_DIM ** -0.5 * LOG2_E * 1.02) * (
        jnp.max(jnp.abs(qg_ref[...])) * jnp.max(jnp.abs(kg_ref[...])))
    bounded = bound <= ATTN_MAX_SCORE_BOUND
    fast, (stage1, stage2, stage2b, stage3), fast_chunk, robust_chunk = _hgrn_ops(
        qf_ref, kk_ref, iv_ref, lf_ref, gt_ref, hg_ref, lbl_ref, oh_ref, tril_ref)
    interleave = jnp.logical_and(bounded, fast)
    n_chunks = SEQ // HGRN_FAST_C
    assert n_chunks == 2 * (SEQ // TQ)

    @pl.when(interleave)
    def _():
        state = [(jnp.zeros((HGRN_K, HGRN_K), F32), jnp.zeros((HGRN_K, HGRN_K), BF16))]
        cumsum, operands, staged = {}, {}, {}

        def hgrn_stages(i):
            if 0 <= i - 3 < n_chunks:
                state[0] = stage3((i - 3) * HGRN_FAST_C, staged.pop(i - 3), *state[0])
            if 0 <= i - 2 < n_chunks:
                staged[i - 2] = stage2b(operands.pop(i - 2))
            if 0 <= i - 1 < n_chunks:
                operands[i - 1] = stage2((i - 1) * HGRN_FAST_C, cumsum.pop(i - 1))
            if i < n_chunks and i % 2 == 0:
                cumsum[i], cumsum[i + 1] = stage1(i * HGRN_FAST_C, 2)

        attend(True, after_block=hgrn_stages)
        for i in range(n_chunks, n_chunks + 3):
            hgrn_stages(i)

    @pl.when(jnp.logical_not(interleave))
    def _():
        @pl.when(bounded)
        def _():
            attend(True)

        @pl.when(jnp.logical_not(bounded))
        def _():
            attend(False)

        st_ref[...] = jnp.zeros(st_ref.shape, F32)

        def looped(chunk_fn, c_len):
            def body(ci, carry):
                st_ref[...] = chunk_fn(pl.multiple_of(ci * c_len, c_len), st_ref[...])
                return carry
            lax.fori_loop(0, SEQ // c_len, body, 0)

        @pl.when(fast)
        def _():
            looped(fast_chunk, HGRN_FAST_C)

        @pl.when(jnp.logical_not(fast))
        def _():
            looped(robust_chunk, HGRN_C)


def _mixer(qa, ka, va, nf, q_g2, k_g2, qf, kk, iv, lf, gt, hgrn_g, lb_logits):
    head_spec = pl.BlockSpec((1, SEQ, LANES), lambda b, h: (b, 0, h))
    gain_spec = pl.BlockSpec((1, LANES), lambda b, h: (0, 0))
    out = jax.ShapeDtypeStruct((BATCH, SEQ, D_ATTN), BF16)
    return pl.pallas_call(
        _mixer_kernel,
        grid=(BATCH, D_ATTN // LANES),
        in_specs=[
            head_spec, head_spec, head_spec,
            pl.BlockSpec((1, N_ATTN_HEADS, SEQ), lambda b, h: (b, 0, 0)),
            gain_spec, gain_spec,
            head_spec, head_spec, head_spec, head_spec, head_spec,
            pl.BlockSpec((1, HGRN_K), lambda b, h: (0, h)),
            pl.BlockSpec((2, HGRN_K), lambda b, h: (0, h)),
        ],
        out_specs=[head_spec, head_spec],
        out_shape=[out, out],
        scratch_shapes=[pltpu.VMEM((HGRN_K, HGRN_K), F32),
                        pltpu.VMEM((HGRN_FAST_C, HGRN_FAST_C), BF16)],
        compiler_params=pltpu.CompilerParams(
            dimension_semantics=("arbitrary", "arbitrary"), vmem_limit_bytes=VMEM_LIMIT),
        name="mixer",
    )(qa, ka, va, nf, q_g2, k_g2, qf, kk, iv, lf, gt, hgrn_g, lb_logits)


def _hgrn_ops(qf_ref, kk_ref, iv_ref, lf_ref, gt_ref, g_ref, lbl_ref, o_ref, tril_ref):
    c_len = HGRN_C
    nt = (((1,), (1,)), ((), ()))
    tn = (((0,), (0,)), ((), ()))

    def finish(o, gate, t0, rows):
        ms = jnp.mean(o * o, axis=-1, keepdims=True)
        y = o * lax.rsqrt(ms + EPS) * g_ref[...]
        o_ref[0, pl.ds(t0, rows), :] = (y * gate.astype(F32)).astype(o_ref.dtype)

    fc = HGRN_FAST_C
    fr = lax.broadcasted_iota(jnp.int32, (fc, fc), 0)
    fcol = lax.broadcasted_iota(jnp.int32, (fc, fc), 1)
    f_tril = fr >= fcol
    tril_ref[...] = f_tril.astype(BF16)

    def fast_stage1(t0, n):
        lf = jnp.concatenate(
            [lf_ref[0, pl.ds(t0 + u * fc, fc), :] for u in range(n)], axis=1)
        hi = lf.astype(BF16)
        r1 = lf - hi.astype(F32)
        mid = r1.astype(BF16)
        lo = (r1 - mid.astype(F32)).astype(BF16)
        tril = tril_ref[...]
        b = (jnp.dot(tril, hi, preferred_element_type=F32)
             + jnp.dot(tril, mid, preferred_element_type=F32)
             + jnp.dot(tril, lo, preferred_element_type=F32))
        return [b[:, u * HGRN_K:(u + 1) * HGRN_K] for u in range(n)]

    def fast_stage2(t0, b):
        qf = qf_ref[0, pl.ds(t0, fc), :].astype(F32)
        kk = kk_ref[0, pl.ds(t0, fc), :].astype(F32)
        iv_t = iv_ref[0, pl.ds(t0, fc), :].astype(F32).T.astype(BF16)
        b_mid = b[fc // 2 - 1:fc // 2, :]
        b_last = b[fc - 1:fc, :]
        d = b - b_mid
        qt = qf * jnp.exp(d)
        kt = kk * jnp.exp(-d)
        qi = (qt * jnp.exp(b_mid)).astype(BF16)
        kd = (kt * jnp.exp(b_last - b_mid)).astype(BF16)
        return qt.astype(BF16), kt.astype(BF16), qi, kd, iv_t, jnp.exp(b_last)

    def fast_stage2b(operands):
        qt, kt, qi, kd, iv_t, st_decay = operands
        a = lax.dot_general(qt, kt, nt, preferred_element_type=F32)
        st_inc = jnp.dot(iv_t, kd, preferred_element_type=F32)
        return a, qi, iv_t, st_inc, st_decay

    def fast_stage3(t0, staged, st, st_bf):
        a, qi, iv_t, st_inc, st_decay = staged
        a = jnp.where(f_tril, a, 0.0).astype(BF16)
        o = lax.dot_general(jnp.concatenate([a, qi], axis=1),
                            jnp.concatenate([iv_t, st_bf], axis=1),
                            nt, preferred_element_type=F32)
        finish(o, gt_ref[0, pl.ds(t0, fc), :], t0, fc)
        st = st * st_decay + st_inc
        return st, st.astype(BF16)

    def fast_chunk(t0, st):
        staged = fast_stage2b(fast_stage2(t0, fast_stage1(t0, 1)[0]))
        return fast_stage3(t0, staged, st, st.astype(BF16))[0]


    row = lax.broadcasted_iota(jnp.int32, (c_len, LANES), 0)
    r_i = lax.broadcasted_iota(jnp.int32, (c_len, c_len), 0)
    c_i = lax.broadcasted_iota(jnp.int32, (c_len, c_len), 1)
    halves = []
    hh = SUBLANES
    while hh < c_len:
        halves.append(hh)
        hh *= 2
    lvl_masks = [
        ((r_i // (2 * h)) == (c_i // (2 * h))) & ((r_i % (2 * h)) >= h) & ((c_i % (2 * h)) < h)
        for h in halves
    ]
    diag_masks = [
        (c_i == (r_i // SUBLANES) * SUBLANES + j) & ((r_i % SUBLANES) >= j)
        for j in range(SUBLANES)
    ]
    def rows_bcast(x, group, idx):
        if group == c_len:
            return jnp.broadcast_to(x[idx:idx + 1, :], x.shape)
        x3 = x.reshape(c_len // group, group, LANES)
        return jnp.broadcast_to(x3[:, idx:idx + 1, :], x3.shape).reshape(c_len, LANES)

    def robust_chunk(t0, st):
        lf = lf_ref[0, pl.ds(t0, c_len), :]
        qf = qf_ref[0, pl.ds(t0, c_len), :].astype(F32)
        kk = kk_ref[0, pl.ds(t0, c_len), :].astype(F32)
        iv = iv_ref[0, pl.ds(t0, c_len), :]
        b = lf
        k = 1
        while k < c_len:
            b = b + jnp.where(row >= k, pltpu.roll(b, k, 0), 0.0)
            k *= 2
        b_last = b[c_len - 1:c_len, :]
        qi = (qf * jnp.exp(b)).astype(BF16)
        o = lax.dot_general(qi, st.astype(BF16), nt, preferred_element_type=F32)
        a = jnp.zeros((c_len, c_len), F32)
        for h, msk in zip(halves, lvl_masks):
            ref = rows_bcast(b, 2 * h, h - 1)
            ql = (qf * jnp.exp(jnp.minimum(b - ref, 0.0))).astype(BF16)
            kl = (kk * jnp.exp(jnp.minimum(ref - b, 0.0))).astype(BF16)
            al = lax.dot_general(ql, kl, nt, preferred_element_type=F32)
            a = jnp.where(msk, al, a)
        for j in range(SUBLANES):
            bs = rows_bcast(b, SUBLANES, j)
            ks = rows_bcast(kk, SUBLANES, j)
            e = jnp.exp(jnp.minimum(b - bs, 0.0))
            col = jnp.sum(qf * ks * e, axis=-1, keepdims=True)
            a = jnp.where(diag_masks[j], col, a)
        o = o + jnp.dot(a.astype(BF16), iv, preferred_element_type=F32)
        finish(o, gt_ref[0, pl.ds(t0, c_len), :], t0, c_len)
        kd = (kk * jnp.exp(b_last - b)).astype(BF16)
        return st * jnp.exp(b_last) + lax.dot_general(iv, kd, tn, preferred_element_type=F32)

    lbl = lbl_ref[...]
    e = jnp.exp(lbl - jnp.max(lbl, axis=0, keepdims=True))
    lb = e[0:1, :] / jnp.sum(e, axis=0, keepdims=True)
    worst_exponent = (HGRN_FAST_C // 2) * jnp.max(-jnp.log(lb))
    fast = worst_exponent <= HGRN_FAST_MAX_EXPONENT
    return fast, (fast_stage1, fast_stage2, fast_stage2b, fast_stage3), fast_chunk, robust_chunk


def kernel(x, c, w_ada, b_ada, g_norm1, ffn1_w_in, ffn1_w_out, g_norm_mix, w_in_mix, b_fgate,
           q_norm_g, k_norm_g, attn_out_g, hgrn_lb_logits, hgrn_out_g, w_out_mix, g_norm2,
           ffn2_w_in, ffn2_w_out):
    assert x.shape == (BATCH, SEQ, D_MODEL) and w_ada.shape[0] == 1
    t = BATCH * SEQ
    mod = _adaln(c, w_ada[0], b_ada).reshape(BATCH, N_MOD, D_MODEL)

    x0 = x.reshape(t, D_MODEL)
    x1 = _ffn(x0, mod, g_norm1, ffn1_w_in[0].astype(BF16), ffn1_w_out[0].astype(BF16), 0, "ffn1")

    w_mix = w_in_mix[0]
    fg0 = 3 * D_ATTN
    w_att = w_mix[:, :fg0].astype(BF16)
    w_hgrn = w_mix[:, fg0 + N_ATTN_HEADS:].astype(BF16)
    w_fg = jnp.pad(w_mix[:, fg0:fg0 + N_ATTN_HEADS], ((0, 0), (0, LANES - N_ATTN_HEADS))).astype(BF16)
    b_fg = jnp.broadcast_to(b_fgate[0][:, None], (N_ATTN_HEADS, LANES))
    q_g2 = jnp.tile(q_norm_g, (1, LANES // ATTN_HEAD_DIM))
    k_g2 = jnp.tile(k_norm_g, (1, LANES // ATTN_HEAD_DIM))
    qa, ka, va, ls, qf, lf, kk, iv, gt = _mix_in(
        x1, mod, g_norm_mix, w_att, w_hgrn, w_fg, b_fg, q_g2, k_g2, hgrn_lb_logits)

    nf = _fcum(ls.reshape(BATCH * N_ATTN_HEADS, SEQ)).reshape(BATCH, N_ATTN_HEADS, SEQ)
    seq3 = lambda a: a.reshape(BATCH, SEQ, a.shape[-1])
    oa, oh = _mixer(seq3(qa), seq3(ka), seq3(va), nf, q_g2, k_g2,
                    seq3(qf), seq3(kk), seq3(iv), seq3(lf), seq3(gt), hgrn_out_g, hgrn_lb_logits)

    mixer_out = (oa.reshape(t, D_ATTN), oh.reshape(t, D_HGRN), attn_out_g, w_out_mix[0].astype(BF16))
    x3 = _ffn(x1, mod, g_norm2, ffn2_w_in[0].astype(BF16), ffn2_w_out[0].astype(BF16), 6, "ffn2",
              mixer_out=mixer_out)
    return x3.reshape(BATCH, SEQ, D_MODEL)
```

```python
import functools

import jax
import jax.numpy as jnp
from jax import lax
from jax.experimental import pallas as pl
from jax.experimental.pallas import tpu as pltpu

D_MODEL = 1024
BATCH = 32
SEQ = 2048
D_ATTN = 512
D_HGRN = 512
ATTN_HEAD_DIM = 64
N_ATTN_HEADS = 8
HGRN_K = 128
N_HGRN_HEADS = 4
D_FF = 2816
N_MOD = 9
FFN_RES = 0.5
EPS = 1e-6
LOG2_E = 1.4426950408889634

LANES = 128
SUBLANES = 8
N_MAIN = 3 * D_ATTN + 4 * D_HGRN

F32 = jnp.float32
BF16 = jnp.bfloat16

TM_FFN = 1024
FFN_SUB = 512
TM_MIX = 1024
MIX_SUB = 256
FCUM_ROWS = 64
TQ = 256
ATTN_BLOCK_ORDER = (7, 6, 5, 4, 3, 2, 1, 0)
ATTN_MAX_SCORE_BOUND = 100.0
HGRN_C = 64
HGRN_FAST_C = 128
HGRN_FAST_UNROLL = 4
HGRN_FAST_MAX_EXPONENT = 70.0
VMEM_LIMIT = 56 * 1024 * 1024


def _sigmoid(x):
    return 1.0 / (1.0 + jnp.exp(-x))


def _silu(x):
    return x * _sigmoid(x)


def _modulated_rms(x, g, shift, scale):
    ms = jnp.mean(x * x, axis=-1, keepdims=True)
    y = x * lax.rsqrt(ms + EPS) * g
    return y * (1.0 + scale) + shift


def _adaln_kernel(c_ref, w_ref, b_ref, o_ref):
    cs = _silu(c_ref[...]).astype(BF16)
    w = w_ref[...].astype(BF16)
    o_ref[...] = jnp.dot(cs, w, preferred_element_type=F32) + b_ref[...]


def _adaln(c, w, b):
    n = w.shape[1]
    tn = D_MODEL
    return pl.pallas_call(
        _adaln_kernel,
        grid=(n // tn,),
        in_specs=[
            pl.BlockSpec((BATCH, D_MODEL), lambda j: (0, 0)),
            pl.BlockSpec((D_MODEL, tn), lambda j: (0, j)),
            pl.BlockSpec((1, tn), lambda j: (0, j)),
        ],
        out_specs=pl.BlockSpec((BATCH, tn), lambda j: (0, j)),
        out_shape=jax.ShapeDtypeStruct((BATCH, n), F32),
        compiler_params=pltpu.CompilerParams(dimension_semantics=("arbitrary",)),
        name="adaln",
    )(c, w, b)


def _ffn_kernel(*refs, mod_base, with_mixer_out):
    if with_mixer_out:
        x_ref, mod_ref, g_ref, win_ref, wout_ref, oa_ref, oh_ref, ag_ref, wmo_ref, o_ref = refs
    else:
        x_ref, mod_ref, g_ref, win_ref, wout_ref, o_ref = refs
    shift = mod_ref[0, mod_base:mod_base + 1, :]
    scale = mod_ref[0, mod_base + 1:mod_base + 2, :]
    gate = mod_ref[0, mod_base + 2:mod_base + 3, :]
    for r in range(TM_FFN // FFN_SUB):
        rows = slice(r * FFN_SUB, (r + 1) * FFN_SUB)
        x = x_ref[rows, :]
        if with_mixer_out:
            oa = oa_ref[rows, :].astype(F32)
            ms = jnp.mean(oa * oa, axis=-1, keepdims=True)
            oa = (oa * lax.rsqrt(ms + EPS) * ag_ref[...]).astype(BF16)
            cat = jnp.concatenate([oa, oh_ref[rows, :]], axis=-1)
            mixed = jnp.dot(cat, wmo_ref[...], preferred_element_type=F32)
            x = x + mod_ref[0, 5:6, :] * mixed
        h = _modulated_rms(x, g_ref[...], shift, scale).astype(BF16)
        gu = jnp.dot(h, win_ref[...], preferred_element_type=F32)
        a = (_silu(gu[:, :D_FF]) * gu[:, D_FF:]).astype(BF16)
        out = jnp.dot(a, wout_ref[...], preferred_element_type=F32)
        o_ref[rows, :] = x + (FFN_RES * gate) * out


def _resident(shape):
    return pl.BlockSpec(shape, lambda i: (0,) * len(shape), pipeline_mode=pl.Buffered(1))


def _ffn(x2d, mod, g, w_in, w_out, mod_base, name, mixer_out=None):
    t = x2d.shape[0]
    tiles_per_seq = SEQ // TM_FFN
    row_blk = lambda n: pl.BlockSpec((TM_FFN, n), lambda i: (i, 0))
    in_specs = [
        row_blk(D_MODEL),
        pl.BlockSpec((1, N_MOD, D_MODEL), lambda i: (i // tiles_per_seq, 0, 0)),
        _resident((1, D_MODEL)),
        _resident((D_MODEL, 2 * D_FF)),
        _resident((D_FF, D_MODEL)),
    ]
    args = [x2d, mod, g, w_in, w_out]
    if mixer_out is not None:
        in_specs += [row_blk(D_ATTN), row_blk(D_HGRN), _resident((1, D_ATTN)),
                     _resident((D_MODEL, D_MODEL))]
        args += list(mixer_out)
    return pl.pallas_call(
        functools.partial(_ffn_kernel, mod_base=mod_base, with_mixer_out=mixer_out is not None),
        grid=(t // TM_FFN,),
        in_specs=in_specs,
        out_specs=row_blk(D_MODEL),
        out_shape=jax.ShapeDtypeStruct((t, D_MODEL), F32),
        compiler_params=pltpu.CompilerParams(
            dimension_semantics=("arbitrary",), vmem_limit_bytes=VMEM_LIMIT),
        name=name,
    )(*args)


def _mix_in_kernel(x_ref, mod_ref, g_ref, w_ref, wfg_ref, bfg_ref, qg_ref, kg_ref, lbl_ref,
                   qa_ref, ka_ref, va_ref, ls_ref, *hgrn_refs):
    for r in range(TM_MIX // MIX_SUB):
        rows = pl.ds(r * MIX_SUB, MIX_SUB)
        _mix_in_rows(x_ref.at[rows, :], mod_ref, g_ref, w_ref, wfg_ref, bfg_ref, qg_ref, kg_ref,
                     lbl_ref, qa_ref.at[rows, :], ka_ref.at[rows, :], va_ref.at[rows, :],
                     ls_ref.at[:, rows], *[ref.at[rows, :] for ref in hgrn_refs])


def _mix_in_rows(x_ref, mod_ref, g_ref, w_ref, wfg_ref, bfg_ref, qg_ref, kg_ref, lbl_ref,
                 qa_ref, ka_ref, va_ref, ls_ref, qf_ref, lf_ref, kk_ref, iv_ref, gt_ref):
    x = x_ref[...]
    h = _modulated_rms(x, g_ref[...], mod_ref[0, 3:4, :], mod_ref[0, 4:5, :]).astype(BF16)
    p = jnp.dot(h, w_ref[...], preferred_element_type=F32)

    lo = lax.broadcasted_iota(jnp.int32, (1, LANES), 1) < ATTN_HEAD_DIM
    inv_dh = 1.0 / ATTN_HEAD_DIM

    def head_norm(t, gain):
        sq = t * t
        s_lo = jnp.sum(jnp.where(lo, sq, 0.0), axis=-1, keepdims=True)
        s_hi = jnp.sum(jnp.where(lo, 0.0, sq), axis=-1, keepdims=True)
        r = jnp.where(lo, lax.rsqrt(s_lo * inv_dh + EPS), lax.rsqrt(s_hi * inv_dh + EPS))
        return (t * r * gain).astype(BF16)

    q_gain = qg_ref[...] * (ATTN_HEAD_DIM ** -0.5 * LOG2_E)
    k_gain = kg_ref[...]
    for j in range(D_ATTN // LANES):
        sl = slice(j * LANES, (j + 1) * LANES)
        qa_ref[:, sl] = head_norm(p[:, j * LANES:(j + 1) * LANES], q_gain)
        ka_ref[:, sl] = head_norm(p[:, D_ATTN + j * LANES:D_ATTN + (j + 1) * LANES], k_gain)
    va_ref[...] = p[:, 2 * D_ATTN:3 * D_ATTN].astype(BF16)

    fg = jnp.dot(h, wfg_ref[...], preferred_element_type=F32)
    fgt = fg.T[:N_ATTN_HEADS, :] + bfg_ref[:, 0:1]
    ls_ref[...] = jnp.minimum(fgt, 0.0) - jnp.log1p(jnp.exp(-jnp.abs(fgt)))

    lbl = lbl_ref[...]
    e = jnp.exp(lbl - jnp.max(lbl, axis=0, keepdims=True))
    lb = e[0:1, :] / jnp.sum(e, axis=0, keepdims=True)
    o0 = 3 * D_ATTN
    qh = p[:, o0:o0 + D_HGRN]
    z = p[:, o0 + D_HGRN:o0 + 2 * D_HGRN]
    ih = p[:, o0 + 2 * D_HGRN:o0 + 3 * D_HGRN]
    gh = p[:, o0 + 3 * D_HGRN:o0 + 4 * D_HGRN]
    t = jnp.exp(-jnp.abs(z))
    r = 1.0 / (1.0 + t)
    tr = t * r
    pos = z >= 0.0
    sig_p = jnp.where(pos, r, tr)
    sig_n = jnp.where(pos, tr, r)
    lf_ref[...] = jnp.log(lb + (1.0 - lb) * sig_p)
    kk_ref[...] = ((1.0 - lb) * sig_n).astype(BF16)
    qf_ref[...] = _silu(qh).astype(BF16)
    iv_ref[...] = ih.astype(BF16)
    gt_ref[...] = _silu(gh).astype(BF16)


def _mix_in(x2d, mod, g, w_main, w_fg, b_fg, q_g2, k_g2, lb_logits):
    t = x2d.shape[0]
    tm = TM_MIX
    tiles_per_seq = SEQ // tm
    row_blk = lambda n: pl.BlockSpec((tm, n), lambda i: (i, 0))
    bf = lambda n: jax.ShapeDtypeStruct((t, n), BF16)
    return pl.pallas_call(
        _mix_in_kernel,
        grid=(t // tm,),
        in_specs=[
            row_blk(D_MODEL),
            pl.BlockSpec((1, N_MOD, D_MODEL), lambda i: (i // tiles_per_seq, 0, 0)),
            _resident((1, D_MODEL)),
            _resident((D_MODEL, N_MAIN)),
            _resident((D_MODEL, LANES)),
            _resident((N_ATTN_HEADS, LANES)),
            _resident((1, LANES)),
            _resident((1, LANES)),
            _resident((2, D_HGRN)),
        ],
        out_specs=[
            row_blk(D_ATTN), row_blk(D_ATTN), row_blk(D_ATTN),
            pl.BlockSpec((None, N_ATTN_HEADS, tm),
                         lambda i: (i // tiles_per_seq, 0, i % tiles_per_seq)),
            row_blk(D_HGRN), row_blk(D_HGRN), row_blk(D_HGRN), row_blk(D_HGRN), row_blk(D_HGRN),
        ],
        out_shape=[
            bf(D_ATTN), bf(D_ATTN), bf(D_ATTN),
            jax.ShapeDtypeStruct((BATCH, N_ATTN_HEADS, SEQ), F32),
            bf(D_HGRN), jax.ShapeDtypeStruct((t, D_HGRN), F32), bf(D_HGRN), bf(D_HGRN), bf(D_HGRN),
        ],
        compiler_params=pltpu.CompilerParams(
            dimension_semantics=("arbitrary",), vmem_limit_bytes=VMEM_LIMIT),
        name="mix_in",
    )(x2d, mod, g, w_main, w_fg, b_fg, q_g2, k_g2, lb_logits)


def _fcum_kernel(ls_ref, nf_ref):
    blk = 256
    r = lax.broadcasted_iota(jnp.int32, (blk, blk), 0)
    c = lax.broadcasted_iota(jnp.int32, (blk, blk), 1)
    upper = (r <= c).astype(F32)
    carry = jnp.zeros((FCUM_ROWS, 1), F32)
    for j in range(SEQ // blk):
        x = ls_ref[:, j * blk:(j + 1) * blk]
        cs = jnp.dot(x, upper, precision=lax.Precision.HIGHEST,
                     preferred_element_type=F32) + carry
        nf_ref[:, j * blk:(j + 1) * blk] = cs * (-LOG2_E)
        carry = cs[:, blk - 1:blk]


def _fcum(ls):
    spec = pl.BlockSpec((FCUM_ROWS, SEQ), lambda i: (i, 0))
    return pl.pallas_call(
        _fcum_kernel,
        grid=(ls.shape[0] // FCUM_ROWS,),
        in_specs=[spec],
        out_specs=spec,
        out_shape=jax.ShapeDtypeStruct(ls.shape, F32),
        compiler_params=pltpu.CompilerParams(dimension_semantics=("arbitrary",)),
        name="fcum",
    )(ls)


def _mixer_kernel(q_ref, k_ref, v_ref, nf_ref, qg_ref, kg_ref,
                  qf_ref, kk_ref, iv_ref, lf_ref, gt_ref, hg_ref, lbl_ref,
                  o_ref, oh_ref, st_ref):
    hp = pl.program_id(1)
    lane = lax.broadcasted_iota(jnp.int32, (1, LANES), 1)
    lo = lane < ATTN_HEAD_DIM
    r_i = lax.broadcasted_iota(jnp.int32, (TQ, TQ), 0)
    c_i = lax.broadcasted_iota(jnp.int32, (TQ, TQ), 1)
    causal = c_i <= r_i
    nt = (((1,), (1,)), ((), ()))
    v1 = jnp.concatenate([v_ref[0], jnp.ones((SEQ, LANES), BF16)], axis=1)

    def attend(bounded, after_block=None):
        if bounded:
            nf8 = nf_ref[0]
            nf_t = jnp.concatenate(
                [nf8, jnp.zeros((LANES - N_ATTN_HEADS, SEQ), F32)], axis=0).T
        def scores(qi, a):
            q0 = qi * TQ
            head = 2 * hp + a
            q = q_ref[0, q0:q0 + TQ, :]
            qa = jnp.where(lo if a == 0 else jnp.logical_not(lo), q, jnp.zeros_like(q))
            nf = nf_ref[0, pl.ds(head, 1), :]
            s_d = lax.dot_general(qa, k_ref[0, q0:q0 + TQ, :], nt,
                                  preferred_element_type=F32) + nf[:, q0:q0 + TQ]
            s_d = jnp.where(causal, s_d, -jnp.inf)
            s_f = None
            if qi > 0:
                s_f = lax.dot_general(qa, k_ref[0, :q0, :], nt,
                                      preferred_element_type=F32) + nf[:, :q0]
            return s_d, s_f

        def values(qi, a, s_d, s_f):
            q0 = qi * TQ
            if bounded:
                m = jnp.sum(jnp.where(lane == 2 * hp + a, nf_t[q0:q0 + TQ, :], 0.0),
                            axis=-1, keepdims=True)
            else:
                m = jnp.max(s_d, axis=-1, keepdims=True)
                if qi > 0:
                    m = jnp.maximum(m, jnp.max(s_f, axis=-1, keepdims=True))
            acc = jnp.dot(jnp.exp2(s_d - m).astype(BF16), v1[q0:q0 + TQ, :],
                          preferred_element_type=F32)
            if qi > 0:
                acc = acc + jnp.dot(jnp.exp2(s_f - m).astype(BF16), v1[:q0, :],
                                    preferred_element_type=F32)
            return acc[:, :LANES] * (1.0 / acc[:, LANES:])

        blocks = [(qi, a) for qi in ATTN_BLOCK_ORDER for a in range(2)]
        pending = scores(*blocks[0])
        outs = []
        for i, (qi, a) in enumerate(blocks):
            current = pending
            if i + 1 < len(blocks):
                pending = scores(*blocks[i + 1])
            outs.append(values(qi, a, *current))
            if after_block is not None:
                after_block(i)
            if a == 1:
                o_ref[0, qi * TQ:(qi + 1) * TQ, :] = jnp.where(
                    lo, outs[0], outs[1]).astype(o_ref.dtype)
                outs = []

    bound = (ATTN_HEAD_DIM * ATTN_HEAD_DIM ** -0.5 * LOG2_E * 1.02) * (
        jnp.max(jnp.abs(qg_ref[...])) * jnp.max(jnp.abs(kg_ref[...])))
    bounded = bound <= ATTN_MAX_SCORE_BOUND
    fast, (stage1, stage2, stage2b, stage3), fast_chunk, robust_chunk = _hgrn_ops(
        qf_ref, kk_ref, iv_ref, lf_ref, gt_ref, hg_ref, lbl_ref, oh_ref)
    interleave = jnp.logical_and(bounded, fast)
    n_chunks = SEQ // HGRN_FAST_C
    assert n_chunks == 2 * (SEQ // TQ)

    @pl.when(interleave)
    def _():
        state = [jnp.zeros((HGRN_K, HGRN_K), F32)]
        cumsum, operands, staged = {}, {}, {}

        def hgrn_stages(i):
            if 0 <= i - 3 < n_chunks:
                state[0] = stage3((i - 3) * HGRN_FAST_C, staged.pop(i - 3), state[0])
            if 0 <= i - 2 < n_chunks:
                staged[i - 2] = stage2b(operands.pop(i - 2))
            if 0 <= i - 1 < n_chunks:
                operands[i - 1] = stage2((i - 1) * HGRN_FAST_C, cumsum.pop(i - 1))
            if i < n_chunks and i % 2 == 0:
                cumsum[i], cumsum[i + 1] = stage1(i * HGRN_FAST_C, 2)

        attend(True, after_block=hgrn_stages)
        for i in range(n_chunks, n_chunks + 3):
            hgrn_stages(i)

    @pl.when(jnp.logical_not(interleave))
    def _():
        @pl.when(bounded)
        def _():
            attend(True)

        @pl.when(jnp.logical_not(bounded))
        def _():
            attend(False)

        st_ref[...] = jnp.zeros(st_ref.shape, F32)

        def looped(chunk_fn, c_len):
            def body(ci, carry):
                st_ref[...] = chunk_fn(pl.multiple_of(ci * c_len, c_len), st_ref[...])
                return carry
            lax.fori_loop(0, SEQ // c_len, body, 0)

        @pl.when(fast)
        def _():
            looped(fast_chunk, HGRN_FAST_C)

        @pl.when(jnp.logical_not(fast))
        def _():
            looped(robust_chunk, HGRN_C)


def _mixer(qa, ka, va, nf, q_g2, k_g2, qf, kk, iv, lf, gt, hgrn_g, lb_logits):
    head_spec = pl.BlockSpec((1, SEQ, LANES), lambda b, h: (b, 0, h))
    gain_spec = pl.BlockSpec((1, LANES), lambda b, h: (0, 0))
    out = jax.ShapeDtypeStruct((BATCH, SEQ, D_ATTN), BF16)
    return pl.pallas_call(
        _mixer_kernel,
        grid=(BATCH, D_ATTN // LANES),
        in_specs=[
            head_spec, head_spec, head_spec,
            pl.BlockSpec((1, N_ATTN_HEADS, SEQ), lambda b, h: (b, 0, 0)),
            gain_spec, gain_spec,
            head_spec, head_spec, head_spec, head_spec, head_spec,
            pl.BlockSpec((1, HGRN_K), lambda b, h: (0, h)),
            pl.BlockSpec((2, HGRN_K), lambda b, h: (0, h)),
        ],
        out_specs=[head_spec, head_spec],
        out_shape=[out, out],
        scratch_shapes=[pltpu.VMEM((HGRN_K, HGRN_K), F32)],
        compiler_params=pltpu.CompilerParams(
            dimension_semantics=("arbitrary", "arbitrary"), vmem_limit_bytes=VMEM_LIMIT),
        name="mixer",
    )(qa, ka, va, nf, q_g2, k_g2, qf, kk, iv, lf, gt, hgrn_g, lb_logits)


def _hgrn_ops(qf_ref, kk_ref, iv_ref, lf_ref, gt_ref, g_ref, lbl_ref, o_ref):
    c_len = HGRN_C
    nt = (((1,), (1,)), ((), ()))
    tn = (((0,), (0,)), ((), ()))

    def finish(o, gate, t0, rows):
        ms = jnp.mean(o * o, axis=-1, keepdims=True)
        y = o * lax.rsqrt(ms + EPS) * g_ref[...]
        o_ref[0, pl.ds(t0, rows), :] = (y * gate.astype(F32)).astype(o_ref.dtype)

    fc = HGRN_FAST_C
    fr = lax.broadcasted_iota(jnp.int32, (fc, fc), 0)
    fcol = lax.broadcasted_iota(jnp.int32, (fc, fc), 1)
    f_tril = fr >= fcol
    f_tril_bf = f_tril.astype(BF16)

    def fast_stage1(t0, n):
        lf = jnp.concatenate(
            [lf_ref[0, pl.ds(t0 + u * fc, fc), :] for u in range(n)], axis=1)
        hi = lf.astype(BF16)
        r1 = lf - hi.astype(F32)
        mid = r1.astype(BF16)
        lo = (r1 - mid.astype(F32)).astype(BF16)
        b = (jnp.dot(f_tril_bf, hi, preferred_element_type=F32)
             + jnp.dot(f_tril_bf, mid, preferred_element_type=F32)
             + jnp.dot(f_tril_bf, lo, preferred_element_type=F32))
        return [b[:, u * HGRN_K:(u + 1) * HGRN_K] for u in range(n)]

    def fast_stage2(t0, b):
        qf = qf_ref[0, pl.ds(t0, fc), :].astype(F32)
        kk = kk_ref[0, pl.ds(t0, fc), :].astype(F32)
        iv_t = iv_ref[0, pl.ds(t0, fc), :].astype(F32).T.astype(BF16)
        b_mid = b[fc // 2 - 1:fc // 2, :]
        b_last = b[fc - 1:fc, :]
        d = b - b_mid
        qt = qf * jnp.exp(d)
        kt = kk * jnp.exp(-d)
        qi = (qt * jnp.exp(b_mid)).astype(BF16)
        kd = (kt * jnp.exp(b_last - b_mid)).astype(BF16)
        return qt.astype(BF16), kt.astype(BF16), qi, kd, iv_t, jnp.exp(b_last)

    def fast_stage2b(operands):
        qt, kt, qi, kd, iv_t, st_decay = operands
        a = lax.dot_general(qt, kt, nt, preferred_element_type=F32)
        st_inc = jnp.dot(iv_t, kd, preferred_element_type=F32)
        return a, qi, iv_t, st_inc, st_decay

    def fast_stage3(t0, staged, st):
        a, qi, iv_t, st_inc, st_decay = staged
        a = jnp.where(f_tril, a, 0.0).astype(BF16)
        o = lax.dot_general(jnp.concatenate([a, qi], axis=1),
                            jnp.concatenate([iv_t, st.astype(BF16)], axis=1),
                            nt, preferred_element_type=F32)
        finish(o, gt_ref[0, pl.ds(t0, fc), :], t0, fc)
        return st * st_decay + st_inc

    def fast_chunk(t0, st):
        return fast_stage3(t0, fast_stage2b(fast_stage2(t0, fast_stage1(t0, 1)[0])), st)


    row = lax.broadcasted_iota(jnp.int32, (c_len, LANES), 0)
    r_i = lax.broadcasted_iota(jnp.int32, (c_len, c_len), 0)
    c_i = lax.broadcasted_iota(jnp.int32, (c_len, c_len), 1)
    halves = []
    hh = SUBLANES
    while hh < c_len:
        halves.append(hh)
        hh *= 2
    lvl_masks = [
        ((r_i // (2 * h)) == (c_i // (2 * h))) & ((r_i % (2 * h)) >= h) & ((c_i % (2 * h)) < h)
        for h in halves
    ]
    diag_masks = [
        (c_i == (r_i // SUBLANES) * SUBLANES + j) & ((r_i % SUBLANES) >= j)
        for j in range(SUBLANES)
    ]
    def rows_bcast(x, group, idx):
        if group == c_len:
            return jnp.broadcast_to(x[idx:idx + 1, :], x.shape)
        x3 = x.reshape(c_len // group, group, LANES)
        return jnp.broadcast_to(x3[:, idx:idx + 1, :], x3.shape).reshape(c_len, LANES)

    def robust_chunk(t0, st):
        lf = lf_ref[0, pl.ds(t0, c_len), :]
        qf = qf_ref[0, pl.ds(t0, c_len), :].astype(F32)
        kk = kk_ref[0, pl.ds(t0, c_len), :].astype(F32)
        iv = iv_ref[0, pl.ds(t0, c_len), :]
        b = lf
        k = 1
        while k < c_len:
            b = b + jnp.where(row >= k, pltpu.roll(b, k, 0), 0.0)
            k *= 2
        b_last = b[c_len - 1:c_len, :]
        qi = (qf * jnp.exp(b)).astype(BF16)
        o = lax.dot_general(qi, st.astype(BF16), nt, preferred_element_type=F32)
        a = jnp.zeros((c_len, c_len), F32)
        for h, msk in zip(halves, lvl_masks):
            ref = rows_bcast(b, 2 * h, h - 1)
            ql = (qf * jnp.exp(jnp.minimum(b - ref, 0.0))).astype(BF16)
            kl = (kk * jnp.exp(jnp.minimum(ref - b, 0.0))).astype(BF16)
            al = lax.dot_general(ql, kl, nt, preferred_element_type=F32)
            a = jnp.where(msk, al, a)
        for j in range(SUBLANES):
            bs = rows_bcast(b, SUBLANES, j)
            ks = rows_bcast(kk, SUBLANES, j)
            e = jnp.exp(jnp.minimum(b - bs, 0.0))
            col = jnp.sum(qf * ks * e, axis=-1, keepdims=True)
            a = jnp.where(diag_masks[j], col, a)
        o = o + jnp.dot(a.astype(BF16), iv, preferred_element_type=F32)
        finish(o, gt_ref[0, pl.ds(t0, c_len), :], t0, c_len)
        kd = (kk * jnp.exp(b_last - b)).astype(BF16)
        return st * jnp.exp(b_last) + lax.dot_general(iv, kd, tn, preferred_element_type=F32)

    lbl = lbl_ref[...]
    e = jnp.exp(lbl - jnp.max(lbl, axis=0, keepdims=True))
    lb = e[0:1, :] / jnp.sum(e, axis=0, keepdims=True)
    worst_exponent = (HGRN_FAST_C // 2) * jnp.max(-jnp.log(lb))
    fast = worst_exponent <= HGRN_FAST_MAX_EXPONENT
    return fast, (fast_stage1, fast_stage2, fast_stage2b, fast_stage3), fast_chunk, robust_chunk


def kernel(x, c, w_ada, b_ada, g_norm1, ffn1_w_in, ffn1_w_out, g_norm_mix, w_in_mix, b_fgate,
           q_norm_g, k_norm_g, attn_out_g, hgrn_lb_logits, hgrn_out_g, w_out_mix, g_norm2,
           ffn2_w_in, ffn2_w_out):
    assert x.shape == (BATCH, SEQ, D_MODEL) and w_ada.shape[0] == 1
    t = BATCH * SEQ
    mod = _adaln(c, w_ada[0], b_ada).reshape(BATCH, N_MOD, D_MODEL)

    x0 = x.reshape(t, D_MODEL)
    x1 = _ffn(x0, mod, g_norm1, ffn1_w_in[0].astype(BF16), ffn1_w_out[0].astype(BF16), 0, "ffn1")

    w_mix = w_in_mix[0]
    fg0 = 3 * D_ATTN
    w_main = jnp.concatenate(
        [w_mix[:, :fg0].astype(BF16), w_mix[:, fg0 + N_ATTN_HEADS:].astype(BF16)], axis=1)
    w_fg = jnp.pad(w_mix[:, fg0:fg0 + N_ATTN_HEADS], ((0, 0), (0, LANES - N_ATTN_HEADS))).astype(BF16)
    b_fg = jnp.broadcast_to(b_fgate[0][:, None], (N_ATTN_HEADS, LANES))
    q_g2 = jnp.tile(q_norm_g, (1, LANES // ATTN_HEAD_DIM))
    k_g2 = jnp.tile(k_norm_g, (1, LANES // ATTN_HEAD_DIM))
    qa, ka, va, ls, qf, lf, kk, iv, gt = _mix_in(
        x1, mod, g_norm_mix, w_main, w_fg, b_fg, q_g2, k_g2, hgrn_lb_logits)

    nf = _fcum(ls.reshape(BATCH * N_ATTN_HEADS, SEQ)).reshape(BATCH, N_ATTN_HEADS, SEQ)
    seq3 = lambda a: a.reshape(BATCH, SEQ, a.shape[-1])
    oa, oh = _mixer(seq3(qa), seq3(ka), seq3(va), nf, q_g2, k_g2,
                    seq3(qf), seq3(kk), seq3(iv), seq3(lf), seq3(gt), hgrn_out_g, hgrn_lb_logits)

    mixer_out = (oa.reshape(t, D_ATTN), oh.reshape(t, D_HGRN), attn_out_g, w_out_mix[0].astype(BF16))
    x3 = _ffn(x1, mod, g_norm2, ffn2_w_in[0].astype(BF16), ffn2_w_out[0].astype(BF16), 6, "ffn2",
              mixer_out=mixer_out)
    return x3.reshape(BATCH, SEQ, D_MODEL)
```

```python
import functools

import jax
import jax.numpy as jnp
from jax import lax
from jax.experimental import pallas as pl
from jax.experimental.pallas import tpu as pltpu

D_MODEL = 1024
BATCH = 32
SEQ = 2048
D_ATTN = 512
D_HGRN = 512
ATTN_HEAD_DIM = 64
N_ATTN_HEADS = 8
HGRN_K = 128
N_HGRN_HEADS = 4
D_FF = 2816
N_MOD = 9
FFN_RES = 0.5
EPS = 1e-6
LOG2_E = 1.4426950408889634

LANES = 128
SUBLANES = 8
N_MAIN = 3 * D_ATTN + 4 * D_HGRN

F32 = jnp.float32
BF16 = jnp.bfloat16

TM_FFN = 1024
FFN_SUB = 256
FFN_SUB_WITH_MIXER_OUT = 512
TM_MIX = 1024
MIX_SUB = 256
FCUM_ROWS = 64
TQ = 256
ATTN_BLOCK_ORDER = (7, 6, 5, 4, 3, 2, 1, 0)
ATTN_MAX_SCORE_BOUND = 100.0
HGRN_C = 64
HGRN_FAST_C = 128
HGRN_FAST_UNROLL = 4
HGRN_FAST_MAX_EXPONENT = 70.0
VMEM_LIMIT = 56 * 1024 * 1024


def _sigmoid(x):
    return 1.0 / (1.0 + jnp.exp(-x))


def _silu(x):
    return x * _sigmoid(x)


def _modulated_rms(x, g, shift, scale):
    ms = jnp.mean(x * x, axis=-1, keepdims=True)
    y = x * lax.rsqrt(ms + EPS) * g
    return y * (1.0 + scale) + shift


def _adaln_kernel(c_ref, w_ref, b_ref, o_ref):
    cs = _silu(c_ref[...]).astype(BF16)
    w = w_ref[...].astype(BF16)
    o_ref[...] = jnp.dot(cs, w, preferred_element_type=F32) + b_ref[...]


def _adaln(c, w, b):
    n = w.shape[1]
    tn = D_MODEL
    return pl.pallas_call(
        _adaln_kernel,
        grid=(n // tn,),
        in_specs=[
            pl.BlockSpec((BATCH, D_MODEL), lambda j: (0, 0)),
            pl.BlockSpec((D_MODEL, tn), lambda j: (0, j)),
            pl.BlockSpec((1, tn), lambda j: (0, j)),
        ],
        out_specs=pl.BlockSpec((BATCH, tn), lambda j: (0, j)),
        out_shape=jax.ShapeDtypeStruct((BATCH, n), F32),
        compiler_params=pltpu.CompilerParams(dimension_semantics=("arbitrary",)),
        name="adaln",
    )(c, w, b)


def _ffn_kernel(*refs, mod_base, with_mixer_out):
    if with_mixer_out:
        x_ref, mod_ref, g_ref, win_ref, wout_ref, oa_ref, oh_ref, ag_ref, wmo_ref, o_ref = refs
    else:
        x_ref, mod_ref, g_ref, win_ref, wout_ref, o_ref = refs
    shift = mod_ref[0, mod_base:mod_base + 1, :]
    scale = mod_ref[0, mod_base + 1:mod_base + 2, :]
    gate = mod_ref[0, mod_base + 2:mod_base + 3, :]
    sub = FFN_SUB_WITH_MIXER_OUT if with_mixer_out else FFN_SUB
    for r in range(TM_FFN // sub):
        rows = slice(r * sub, (r + 1) * sub)
        x = x_ref[rows, :]
        if with_mixer_out:
            oa = oa_ref[rows, :].astype(F32)
            ms = jnp.mean(oa * oa, axis=-1, keepdims=True)
            oa = (oa * lax.rsqrt(ms + EPS) * ag_ref[...]).astype(BF16)
            cat = jnp.concatenate([oa, oh_ref[rows, :]], axis=-1)
            mixed = jnp.dot(cat, wmo_ref[...], preferred_element_type=F32)
            x = x + mod_ref[0, 5:6, :] * mixed
        h = _modulated_rms(x, g_ref[...], shift, scale).astype(BF16)
        gu = jnp.dot(h, win_ref[...], preferred_element_type=F32)
        a = (_silu(gu[:, :D_FF]) * gu[:, D_FF:]).astype(BF16)
        out = jnp.dot(a, wout_ref[...], preferred_element_type=F32)
        o_ref[rows, :] = x + (FFN_RES * gate) * out


def _resident(shape):
    return pl.BlockSpec(shape, lambda i: (0,) * len(shape), pipeline_mode=pl.Buffered(1))


def _ffn(x2d, mod, g, w_in, w_out, mod_base, name, mixer_out=None):
    t = x2d.shape[0]
    tiles_per_seq = SEQ // TM_FFN
    row_blk = lambda n: pl.BlockSpec((TM_FFN, n), lambda i: (i, 0))
    in_specs = [
        row_blk(D_MODEL),
        pl.BlockSpec((1, N_MOD, D_MODEL), lambda i: (i // tiles_per_seq, 0, 0)),
        _resident((1, D_MODEL)),
        _resident((D_MODEL, 2 * D_FF)),
        _resident((D_FF, D_MODEL)),
    ]
    args = [x2d, mod, g, w_in, w_out]
    if mixer_out is not None:
        in_specs += [row_blk(D_ATTN), row_blk(D_HGRN), _resident((1, D_ATTN)),
                     _resident((D_MODEL, D_MODEL))]
        args += list(mixer_out)
    return pl.pallas_call(
        functools.partial(_ffn_kernel, mod_base=mod_base, with_mixer_out=mixer_out is not None),
        grid=(t // TM_FFN,),
        in_specs=in_specs,
        out_specs=row_blk(D_MODEL),
        out_shape=jax.ShapeDtypeStruct((t, D_MODEL), F32),
        compiler_params=pltpu.CompilerParams(
            dimension_semantics=("arbitrary",), vmem_limit_bytes=VMEM_LIMIT),
        name=name,
    )(*args)


def _mix_in_kernel(x_ref, mod_ref, g_ref, w_ref, wfg_ref, bfg_ref, qg_ref, kg_ref, lbl_ref,
                   qa_ref, ka_ref, va_ref, ls_ref, *hgrn_refs):
    for r in range(TM_MIX // MIX_SUB):
        rows = pl.ds(r * MIX_SUB, MIX_SUB)
        _mix_in_rows(x_ref.at[rows, :], mod_ref, g_ref, w_ref, wfg_ref, bfg_ref, qg_ref, kg_ref,
                     lbl_ref, qa_ref.at[rows, :], ka_ref.at[rows, :], va_ref.at[rows, :],
                     ls_ref.at[:, rows], *[ref.at[rows, :] for ref in hgrn_refs])


def _mix_in_rows(x_ref, mod_ref, g_ref, w_ref, wfg_ref, bfg_ref, qg_ref, kg_ref, lbl_ref,
                 qa_ref, ka_ref, va_ref, ls_ref, qf_ref, lf_ref, kk_ref, iv_ref, gt_ref):
    x = x_ref[...]
    h = _modulated_rms(x, g_ref[...], mod_ref[0, 3:4, :], mod_ref[0, 4:5, :]).astype(BF16)
    p = jnp.dot(h, w_ref[...], preferred_element_type=F32)

    lo = lax.broadcasted_iota(jnp.int32, (1, LANES), 1) < ATTN_HEAD_DIM
    inv_dh = 1.0 / ATTN_HEAD_DIM

    def head_norm(t, gain):
        sq = t * t
        s_lo = jnp.sum(jnp.where(lo, sq, 0.0), axis=-1, keepdims=True)
        s_hi = jnp.sum(jnp.where(lo, 0.0, sq), axis=-1, keepdims=True)
        r = jnp.where(lo, lax.rsqrt(s_lo * inv_dh + EPS), lax.rsqrt(s_hi * inv_dh + EPS))
        return (t * r * gain).astype(BF16)

    q_gain = qg_ref[...] * (ATTN_HEAD_DIM ** -0.5 * LOG2_E)
    k_gain = kg_ref[...]
    for j in range(D_ATTN // LANES):
        sl = slice(j * LANES, (j + 1) * LANES)
        qa_ref[:, sl] = head_norm(p[:, j * LANES:(j + 1) * LANES], q_gain)
        ka_ref[:, sl] = head_norm(p[:, D_ATTN + j * LANES:D_ATTN + (j + 1) * LANES], k_gain)
    va_ref[...] = p[:, 2 * D_ATTN:3 * D_ATTN].astype(BF16)

    fg = jnp.dot(h, wfg_ref[...], preferred_element_type=F32)
    fgt = fg.T[:N_ATTN_HEADS, :] + bfg_ref[:, 0:1]
    ls_ref[...] = jnp.minimum(fgt, 0.0) - jnp.log1p(jnp.exp(-jnp.abs(fgt)))

    lbl = lbl_ref[...]
    e = jnp.exp(lbl - jnp.max(lbl, axis=0, keepdims=True))
    lb = e[0:1, :] / jnp.sum(e, axis=0, keepdims=True)
    o0 = 3 * D_ATTN
    qh = p[:, o0:o0 + D_HGRN]
    z = p[:, o0 + D_HGRN:o0 + 2 * D_HGRN]
    ih = p[:, o0 + 2 * D_HGRN:o0 + 3 * D_HGRN]
    gh = p[:, o0 + 3 * D_HGRN:o0 + 4 * D_HGRN]
    t = jnp.exp(-jnp.abs(z))
    r = 1.0 / (1.0 + t)
    tr = t * r
    pos = z >= 0.0
    sig_p = jnp.where(pos, r, tr)
    sig_n = jnp.where(pos, tr, r)
    lf_ref[...] = jnp.log(lb + (1.0 - lb) * sig_p)
    kk_ref[...] = ((1.0 - lb) * sig_n).astype(BF16)
    qf_ref[...] = _silu(qh).astype(BF16)
    iv_ref[...] = ih.astype(BF16)
    gt_ref[...] = _silu(gh).astype(BF16)


def _mix_in(x2d, mod, g, w_main, w_fg, b_fg, q_g2, k_g2, lb_logits):
    t = x2d.shape[0]
    tm = TM_MIX
    tiles_per_seq = SEQ // tm
    row_blk = lambda n: pl.BlockSpec((tm, n), lambda i: (i, 0))
    bf = lambda n: jax.ShapeDtypeStruct((t, n), BF16)
    return pl.pallas_call(
        _mix_in_kernel,
        grid=(t // tm,),
        in_specs=[
            row_blk(D_MODEL),
            pl.BlockSpec((1, N_MOD, D_MODEL), lambda i: (i // tiles_per_seq, 0, 0)),
            _resident((1, D_MODEL)),
            _resident((D_MODEL, N_MAIN)),
            _resident((D_MODEL, LANES)),
            _resident((N_ATTN_HEADS, LANES)),
            _resident((1, LANES)),
            _resident((1, LANES)),
            _resident((2, D_HGRN)),
        ],
        out_specs=[
            row_blk(D_ATTN), row_blk(D_ATTN), row_blk(D_ATTN),
            pl.BlockSpec((None, N_ATTN_HEADS, tm),
                         lambda i: (i // tiles_per_seq, 0, i % tiles_per_seq)),
            row_blk(D_HGRN), row_blk(D_HGRN), row_blk(D_HGRN), row_blk(D_HGRN), row_blk(D_HGRN),
        ],
        out_shape=[
            bf(D_ATTN), bf(D_ATTN), bf(D_ATTN),
            jax.ShapeDtypeStruct((BATCH, N_ATTN_HEADS, SEQ), F32),
            bf(D_HGRN), jax.ShapeDtypeStruct((t, D_HGRN), F32), bf(D_HGRN), bf(D_HGRN), bf(D_HGRN),
        ],
        compiler_params=pltpu.CompilerParams(
            dimension_semantics=("arbitrary",), vmem_limit_bytes=VMEM_LIMIT),
        name="mix_in",
    )(x2d, mod, g, w_main, w_fg, b_fg, q_g2, k_g2, lb_logits)


def _fcum_kernel(ls_ref, nf_ref):
    blk = 256
    r = lax.broadcasted_iota(jnp.int32, (blk, blk), 0)
    c = lax.broadcasted_iota(jnp.int32, (blk, blk), 1)
    upper = (r <= c).astype(F32)
    carry = jnp.zeros((FCUM_ROWS, 1), F32)
    for j in range(SEQ // blk):
        x = ls_ref[:, j * blk:(j + 1) * blk]
        cs = jnp.dot(x, upper, precision=lax.Precision.HIGHEST,
                     preferred_element_type=F32) + carry
        nf_ref[:, j * blk:(j + 1) * blk] = cs * (-LOG2_E)
        carry = cs[:, blk - 1:blk]


def _fcum(ls):
    spec = pl.BlockSpec((FCUM_ROWS, SEQ), lambda i: (i, 0))
    return pl.pallas_call(
        _fcum_kernel,
        grid=(ls.shape[0] // FCUM_ROWS,),
        in_specs=[spec],
        out_specs=spec,
        out_shape=jax.ShapeDtypeStruct(ls.shape, F32),
        compiler_params=pltpu.CompilerParams(dimension_semantics=("arbitrary",)),
        name="fcum",
    )(ls)


def _mixer_kernel(q_ref, k_ref, v_ref, nf_ref, qg_ref, kg_ref,
                  qf_ref, kk_ref, iv_ref, lf_ref, gt_ref, hg_ref, lbl_ref,
                  o_ref, oh_ref, st_ref):
    hp = pl.program_id(1)
    lane = lax.broadcasted_iota(jnp.int32, (1, LANES), 1)
    lo = lane < ATTN_HEAD_DIM
    r_i = lax.broadcasted_iota(jnp.int32, (TQ, TQ), 0)
    c_i = lax.broadcasted_iota(jnp.int32, (TQ, TQ), 1)
    causal = c_i <= r_i
    nt = (((1,), (1,)), ((), ()))
    v1 = jnp.concatenate([v_ref[0], jnp.ones((SEQ, LANES), BF16)], axis=1)

    def attend(bounded, after_block=None):
        if bounded:
            nf8 = nf_ref[0]
            nf_t = jnp.concatenate(
                [nf8, jnp.zeros((LANES - N_ATTN_HEADS, SEQ), F32)], axis=0).T
        def scores(qi, a):
            q0 = qi * TQ
            head = 2 * hp + a
            q = q_ref[0, q0:q0 + TQ, :]
            qa = jnp.where(lo if a == 0 else jnp.logical_not(lo), q, jnp.zeros_like(q))
            nf = nf_ref[0, pl.ds(head, 1), :]
            s_d = lax.dot_general(qa, k_ref[0, q0:q0 + TQ, :], nt,
                                  preferred_element_type=F32) + nf[:, q0:q0 + TQ]
            s_d = jnp.where(causal, s_d, -jnp.inf)
            s_f = None
            if qi > 0:
                s_f = lax.dot_general(qa, k_ref[0, :q0, :], nt,
                                      preferred_element_type=F32) + nf[:, :q0]
            return s_d, s_f

        def values(qi, a, s_d, s_f):
            q0 = qi * TQ
            if bounded:
                m = jnp.sum(jnp.where(lane == 2 * hp + a, nf_t[q0:q0 + TQ, :], 0.0),
                            axis=-1, keepdims=True)
            else:
                m = jnp.max(s_d, axis=-1, keepdims=True)
                if qi > 0:
                    m = jnp.maximum(m, jnp.max(s_f, axis=-1, keepdims=True))
            acc = jnp.dot(jnp.exp2(s_d - m).astype(BF16), v1[q0:q0 + TQ, :],
                          preferred_element_type=F32)
            if qi > 0:
                acc = acc + jnp.dot(jnp.exp2(s_f - m).astype(BF16), v1[:q0, :],
                                    preferred_element_type=F32)
            return acc[:, :LANES] * (1.0 / acc[:, LANES:])

        blocks = [(qi, a) for qi in ATTN_BLOCK_ORDER for a in range(2)]
        pending = scores(*blocks[0])
        outs = []
        for i, (qi, a) in enumerate(blocks):
            current = pending
            if i + 1 < len(blocks):
                pending = scores(*blocks[i + 1])
            outs.append(values(qi, a, *current))
            if after_block is not None:
                after_block(i)
            if a == 1:
                o_ref[0, qi * TQ:(qi + 1) * TQ, :] = jnp.where(
                    lo, outs[0], outs[1]).astype(o_ref.dtype)
                outs = []

    bound = (ATTN_HEAD_DIM * ATTN_HEAD_DIM ** -0.5 * LOG2_E * 1.02) * (
        jnp.max(jnp.abs(qg_ref[...])) * jnp.max(jnp.abs(kg_ref[...])))
    bounded = bound <= ATTN_MAX_SCORE_BOUND
    fast, (stage1, stage2, stage2b, stage3), fast_chunk, robust_chunk = _hgrn_ops(
        qf_ref, kk_ref, iv_ref, lf_ref, gt_ref, hg_ref, lbl_ref, oh_ref)
    interleave = jnp.logical_and(bounded, fast)
    n_chunks = SEQ // HGRN_FAST_C
    assert n_chunks == 2 * (SEQ // TQ)

    @pl.when(interleave)
    def _():
        state = [jnp.zeros((HGRN_K, HGRN_K), F32)]
        cumsum, operands, staged = {}, {}, {}

        def hgrn_stages(i):
            if 0 <= i - 3 < n_chunks:
                state[0] = stage3((i - 3) * HGRN_FAST_C, staged.pop(i - 3), state[0])
            if 0 <= i - 2 < n_chunks:
                staged[i - 2] = stage2b(operands.pop(i - 2))
            if 0 <= i - 1 < n_chunks:
                operands[i - 1] = stage2((i - 1) * HGRN_FAST_C, cumsum.pop(i - 1))
            if i < n_chunks and i % 2 == 0:
                cumsum[i], cumsum[i + 1] = stage1(i * HGRN_FAST_C, 2)

        attend(True, after_block=hgrn_stages)
        for i in range(n_chunks, n_chunks + 3):
            hgrn_stages(i)

    @pl.when(jnp.logical_not(interleave))
    def _():
        @pl.when(bounded)
        def _():
            attend(True)

        @pl.when(jnp.logical_not(bounded))
        def _():
            attend(False)

        st_ref[...] = jnp.zeros(st_ref.shape, F32)

        def looped(chunk_fn, c_len):
            def body(ci, carry):
                st_ref[...] = chunk_fn(pl.multiple_of(ci * c_len, c_len), st_ref[...])
                return carry
            lax.fori_loop(0, SEQ // c_len, body, 0)

        @pl.when(fast)
        def _():
            looped(fast_chunk, HGRN_FAST_C)

        @pl.when(jnp.logical_not(fast))
        def _():
            looped(robust_chunk, HGRN_C)


def _mixer(qa, ka, va, nf, q_g2, k_g2, qf, kk, iv, lf, gt, hgrn_g, lb_logits):
    head_spec = pl.BlockSpec((1, SEQ, LANES), lambda b, h: (b, 0, h))
    gain_spec = pl.BlockSpec((1, LANES), lambda b, h: (0, 0))
    out = jax.ShapeDtypeStruct((BATCH, SEQ, D_ATTN), BF16)
    return pl.pallas_call(
        _mixer_kernel,
        grid=(BATCH, D_ATTN // LANES),
        in_specs=[
            head_spec, head_spec, head_spec,
            pl.BlockSpec((1, N_ATTN_HEADS, SEQ), lambda b, h: (b, 0, 0)),
            gain_spec, gain_spec,
            head_spec, head_spec, head_spec, head_spec, head_spec,
            pl.BlockSpec((1, HGRN_K), lambda b, h: (0, h)),
            pl.BlockSpec((2, HGRN_K), lambda b, h: (0, h)),
        ],
        out_specs=[head_spec, head_spec],
        out_shape=[out, out],
        scratch_shapes=[pltpu.VMEM((HGRN_K, HGRN_K), F32)],
        compiler_params=pltpu.CompilerParams(
            dimension_semantics=("arbitrary", "arbitrary"), vmem_limit_bytes=VMEM_LIMIT),
        name="mixer",
    )(qa, ka, va, nf, q_g2, k_g2, qf, kk, iv, lf, gt, hgrn_g, lb_logits)


def _hgrn_ops(qf_ref, kk_ref, iv_ref, lf_ref, gt_ref, g_ref, lbl_ref, o_ref):
    c_len = HGRN_C
    nt = (((1,), (1,)), ((), ()))
    tn = (((0,), (0,)), ((), ()))

    def finish(o, gate, t0, rows):
        ms = jnp.mean(o * o, axis=-1, keepdims=True)
        y = o * lax.rsqrt(ms + EPS) * g_ref[...]
        o_ref[0, pl.ds(t0, rows), :] = (y * gate.astype(F32)).astype(o_ref.dtype)

    fc = HGRN_FAST_C
    fr = lax.broadcasted_iota(jnp.int32, (fc, fc), 0)
    fcol = lax.broadcasted_iota(jnp.int32, (fc, fc), 1)
    f_tril = fr >= fcol
    f_tril_bf = f_tril.astype(BF16)

    def fast_stage1(t0, n):
        lf = jnp.concatenate(
            [lf_ref[0, pl.ds(t0 + u * fc, fc), :] for u in range(n)], axis=1)
        hi = lf.astype(BF16)
        r1 = lf - hi.astype(F32)
        mid = r1.astype(BF16)
        lo = (r1 - mid.astype(F32)).astype(BF16)
        b = (jnp.dot(f_tril_bf, hi, preferred_element_type=F32)
             + jnp.dot(f_tril_bf, mid, preferred_element_type=F32)
             + jnp.dot(f_tril_bf, lo, preferred_element_type=F32))
        return [b[:, u * HGRN_K:(u + 1) * HGRN_K] for u in range(n)]

    def fast_stage2(t0, b):
        qf = qf_ref[0, pl.ds(t0, fc), :].astype(F32)
        kk = kk_ref[0, pl.ds(t0, fc), :].astype(F32)
        iv_t = iv_ref[0, pl.ds(t0, fc), :].astype(F32).T.astype(BF16)
        b_mid = b[fc // 2 - 1:fc // 2, :]
        b_last = b[fc - 1:fc, :]
        d = b - b_mid
        qt = qf * jnp.exp(d)
        kt = kk * jnp.exp(-d)
        qi = (qt * jnp.exp(b_mid)).astype(BF16)
        kd = (kt * jnp.exp(b_last - b_mid)).astype(BF16)
        return qt.astype(BF16), kt.astype(BF16), qi, kd, iv_t, jnp.exp(b_last)

    def fast_stage2b(operands):
        qt, kt, qi, kd, iv_t, st_decay = operands
        a = lax.dot_general(qt, kt, nt, preferred_element_type=F32)
        st_inc = jnp.dot(iv_t, kd, preferred_element_type=F32)
        return a, qi, iv_t, st_inc, st_decay

    def fast_stage3(t0, staged, st):
        a, qi, iv_t, st_inc, st_decay = staged
        a = jnp.where(f_tril, a, 0.0).astype(BF16)
        o = lax.dot_general(jnp.concatenate([a, qi], axis=1),
                            jnp.concatenate([iv_t, st.astype(BF16)], axis=1),
                            nt, preferred_element_type=F32)
        finish(o, gt_ref[0, pl.ds(t0, fc), :], t0, fc)
        return st * st_decay + st_inc

    def fast_chunk(t0, st):
        return fast_stage3(t0, fast_stage2b(fast_stage2(t0, fast_stage1(t0, 1)[0])), st)


    row = lax.broadcasted_iota(jnp.int32, (c_len, LANES), 0)
    r_i = lax.broadcasted_iota(jnp.int32, (c_len, c_len), 0)
    c_i = lax.broadcasted_iota(jnp.int32, (c_len, c_len), 1)
    halves = []
    hh = SUBLANES
    while hh < c_len:
        halves.append(hh)
        hh *= 2
    lvl_masks = [
        ((r_i // (2 * h)) == (c_i // (2 * h))) & ((r_i % (2 * h)) >= h) & ((c_i % (2 * h)) < h)
        for h in halves
    ]
    diag_masks = [
        (c_i == (r_i // SUBLANES) * SUBLANES + j) & ((r_i % SUBLANES) >= j)
        for j in range(SUBLANES)
    ]
    def rows_bcast(x, group, idx):
        if group == c_len:
            return jnp.broadcast_to(x[idx:idx + 1, :], x.shape)
        x3 = x.reshape(c_len // group, group, LANES)
        return jnp.broadcast_to(x3[:, idx:idx + 1, :], x3.shape).reshape(c_len, LANES)

    def robust_chunk(t0, st):
        lf = lf_ref[0, pl.ds(t0, c_len), :]
        qf = qf_ref[0, pl.ds(t0, c_len), :].astype(F32)
        kk = kk_ref[0, pl.ds(t0, c_len), :].astype(F32)
        iv = iv_ref[0, pl.ds(t0, c_len), :]
        b = lf
        k = 1
        while k < c_len:
            b = b + jnp.where(row >= k, pltpu.roll(b, k, 0), 0.0)
            k *= 2
        b_last = b[c_len - 1:c_len, :]
        qi = (qf * jnp.exp(b)).astype(BF16)
        o = lax.dot_general(qi, st.astype(BF16), nt, preferred_element_type=F32)
        a = jnp.zeros((c_len, c_len), F32)
        for h, msk in zip(halves, lvl_masks):
            ref = rows_bcast(b, 2 * h, h - 1)
            ql = (qf * jnp.exp(jnp.minimum(b - ref, 0.0))).astype(BF16)
            kl = (kk * jnp.exp(jnp.minimum(ref - b, 0.0))).astype(BF16)
            al = lax.dot_general(ql, kl, nt, preferred_element_type=F32)
            a = jnp.where(msk, al, a)
        for j in range(SUBLANES):
            bs = rows_bcast(b, SUBLANES, j)
            ks = rows_bcast(kk, SUBLANES, j)
            e = jnp.exp(jnp.minimum(b - bs, 0.0))
            col = jnp.sum(qf * ks * e, axis=-1, keepdims=True)
            a = jnp.where(diag_masks[j], col, a)
        o = o + jnp.dot(a.astype(BF16), iv, preferred_element_type=F32)
        finish(o, gt_ref[0, pl.ds(t0, c_len), :], t0, c_len)
        kd = (kk * jnp.exp(b_last - b)).astype(BF16)
        return st * jnp.exp(b_last) + lax.dot_general(iv, kd, tn, preferred_element_type=F32)

    lbl = lbl_ref[...]
    e = jnp.exp(lbl - jnp.max(lbl, axis=0, keepdims=True))
    lb = e[0:1, :] / jnp.sum(e, axis=0, keepdims=True)
    worst_exponent = (HGRN_FAST_C // 2) * jnp.max(-jnp.log(lb))
    fast = worst_exponent <= HGRN_FAST_MAX_EXPONENT
    return fast, (fast_stage1, fast_stage2, fast_stage2b, fast_stage3), fast_chunk, robust_chunk


def kernel(x, c, w_ada, b_ada, g_norm1, ffn1_w_in, ffn1_w_out, g_norm_mix, w_in_mix, b_fgate,
           q_norm_g, k_norm_g, attn_out_g, hgrn_lb_logits, hgrn_out_g, w_out_mix, g_norm2,
           ffn2_w_in, ffn2_w_out):
    assert x.shape == (BATCH, SEQ, D_MODEL) and w_ada.shape[0] == 1
    t = BATCH * SEQ
    mod = _adaln(c, w_ada[0], b_ada).reshape(BATCH, N_MOD, D_MODEL)

    x0 = x.reshape(t, D_MODEL)
    x1 = _ffn(x0, mod, g_norm1, ffn1_w_in[0].astype(BF16), ffn1_w_out[0].astype(BF16), 0, "ffn1")

    w_mix = w_in_mix[0]
    fg0 = 3 * D_ATTN
    w_main = jnp.concatenate(
        [w_mix[:, :fg0].astype(BF16), w_mix[:, fg0 + N_ATTN_HEADS:].astype(BF16)], axis=1)
    w_fg = jnp.pad(w_mix[:, fg0:fg0 + N_ATTN_HEADS], ((0, 0), (0, LANES - N_ATTN_HEADS))).astype(BF16)
    b_fg = jnp.broadcast_to(b_fgate[0][:, None], (N_ATTN_HEADS, LANES))
    q_g2 = jnp.tile(q_norm_g, (1, LANES // ATTN_HEAD_DIM))
    k_g2 = jnp.tile(k_norm_g, (1, LANES // ATTN_HEAD_DIM))
    qa, ka, va, ls, qf, lf, kk, iv, gt = _mix_in(
        x1, mod, g_norm_mix, w_main, w_fg, b_fg, q_g2, k_g2, hgrn_lb_logits)

    nf = _fcum(ls.reshape(BATCH * N_ATTN_HEADS, SEQ)).reshape(BATCH, N_ATTN_HEADS, SEQ)
    seq3 = lambda a: a.reshape(BATCH, SEQ, a.shape[-1])
    oa, oh = _mixer(seq3(qa), seq3(ka), seq3(va), nf, q_g2, k_g2,
                    seq3(qf), seq3(kk), seq3(iv), seq3(lf), seq3(gt), hgrn_out_g, hgrn_lb_logits)

    mixer_out = (oa.reshape(t, D_ATTN), oh.reshape(t, D_HGRN), attn_out_g, w_out_mix[0].astype(BF16))
    x3 = _ffn(x1, mod, g_norm2, ffn2_w_in[0].astype(BF16), ffn2_w_out[0].astype(BF16), 6, "ffn2",
              mixer_out=mixer_out)
    return x3.reshape(BATCH, SEQ, D_MODEL)
```
